```python
import math
import jax
import jax.numpy as jnp
from jax import lax
import numpy as np

D_MODEL = 1024
BATCH = 4
SEQ = 4096
DEPTH = 4
DEC_BATCH = 32
DEC_SEQ = 8
PAST_LEN = 8192
PAGE_SIZE = 128

N_MIXERS = 3
N_A = (DEPTH + N_MIXERS - 1) // N_MIXERS
N_B = (DEPTH + N_MIXERS - 2) // N_MIXERS
N_C = DEPTH // N_MIXERS
HEAD_DIM = 64
ROPE_THETA = 500000.0
Q_BLOCK = 128
LN_EPS = 1e-5
DEEPNORM_ALPHA = (2 * DEPTH) ** 0.25
DEEPNORM_BETA = (8 * DEPTH) ** -0.25

A_GROUPS = ((128, 1), (512, 4), (2048, 16))
A_HEADS = 8
A_WIDTH = A_HEADS * HEAD_DIM
A_IN = len(A_GROUPS) * 3 * A_WIDTH
B_HEADS = 8
B_VDIM = 2 * HEAD_DIM
B_IN = 3 * B_HEADS * B_VDIM
C_HEADS = 16
C_KV_HEADS = 4
IDX_HEADS = 8
IDX_DIM = 64
C_TOPK_MAX = 256
C_SPLITS = (C_HEADS * HEAD_DIM,
            C_HEADS * HEAD_DIM + C_KV_HEADS * HEAD_DIM,
            C_HEADS * HEAD_DIM + 2 * C_KV_HEADS * HEAD_DIM,
            C_HEADS * HEAD_DIM + 2 * C_KV_HEADS * HEAD_DIM + IDX_HEADS * IDX_DIM,
            C_HEADS * HEAD_DIM + 2 * C_KV_HEADS * HEAD_DIM + IDX_HEADS * IDX_DIM + IDX_DIM)
C_IN = C_SPLITS[-1] + IDX_HEADS
IDX_W_SCALE = IDX_HEADS ** -0.5 * IDX_DIM ** -0.5
FFN_HIDDEN = -(-8 * D_MODEL // (3 * 256)) * 256

kernel_name = 'hybrid_dilated_diff_dsa_decoder_step'


def _layer_norm(x, g, b):
    xf = x.astype(jnp.float32)
    mu = jnp.mean(xf, axis=-1, keepdims=True)
    var = jnp.mean(jnp.square(xf - mu), axis=-1, keepdims=True)
    return ((xf - mu) * lax.rsqrt(var + LN_EPS) * g + b).astype(x.dtype)


def _rope(x, pos):
    rot = x.shape[-1] // 4
    half = rot // 2
    inv = ROPE_THETA ** (-jnp.arange(0, rot, 2, dtype=jnp.float32) / rot)
    ang = pos.astype(jnp.float32)[:, None] * inv[None, :]
    cos = jnp.cos(ang)[:, None, :]
    sin = jnp.sin(ang)[:, None, :]
    xr = x[..., :rot].astype(jnp.float32)
    x1, x2 = xr[..., :half], xr[..., half:]
    xr = jnp.concatenate([x1 * cos - x2 * sin, x2 * cos + x1 * sin], axis=-1).astype(x.dtype)
    return jnp.concatenate([xr, x[..., rot:]], axis=-1)


def _map_query_blocks(fn, n_q):
    nq = Q_BLOCK if n_q % Q_BLOCK == 0 else n_q
    starts = jnp.arange(n_q // nq, dtype=jnp.int32) * nq
    out = lax.map(lambda s: fn(s, nq), starts)
    return jax.tree_util.tree_map(
        lambda o: jnp.moveaxis(o, 0, 1).reshape((o.shape[1], n_q) + o.shape[3:]), out)


def _dilated_group(q, k, v, dil, window, offset):
    steps = jnp.arange(window // dil + 1, dtype=jnp.int32) * dil
    scale = HEAD_DIM ** -0.5

    def block(s, nq):
        qb = lax.dynamic_slice_in_dim(q, s, nq, axis=1)
        rows = offset + s + jnp.arange(nq, dtype=jnp.int32)[:, None] - steps[None, :]
        valid = rows >= 0
        rows = jnp.maximum(rows, 0)
        kg = k[:, rows]
        vg = v[:, rows]
        sc = jnp.einsum('bqhd,bqnhd->bhqn', qb, kg).astype(jnp.float32) * scale
        sc = jnp.where(valid[None, None], sc, -jnp.inf)
        lse = jax.nn.logsumexp(sc, axis=-1)
        p = jnp.exp(sc - lse[..., None]).astype(v.dtype)
        o = jnp.einsum('bhqn,bqnhd->bqhd', p, vg)
        return o, jnp.moveaxis(lse, 1, 2)

    return _map_query_blocks(block, q.shape[1])


def _mixer_a(h, w_in, w_out, pos, buffers):
    bsz, L, _ = h.shape
    proj = (h @ w_in).reshape(bsz, L, len(A_GROUPS), 3, A_HEADS, HEAD_DIM)
    outs, lses, new_bufs = [], [], []
    for g, (window, dil) in enumerate(A_GROUPS):
        q = _rope(proj[:, :, g, 0], pos)
        k = _rope(proj[:, :, g, 1], pos)
        kv = jnp.stack([k, proj[:, :, g, 2]], axis=2)
        if buffers is None:
            full, offset, keep = kv, 0, min(window, L)
        else:
            offset = buffers[g].shape[1]
            full, keep = jnp.concatenate([buffers[g], kv], axis=1), offset
        o, lse = _dilated_group(q, full[:, :, 0], full[:, :, 1], dil, window, offset)
        outs.append(o)
        lses.append(lse)
        new_bufs.append(full[:, full.shape[1] - keep:])
    outs = jnp.stack(outs)
    wts = jax.nn.softmax(jnp.stack(lses), axis=0)
    o = jnp.einsum('gblh,gblhd->blhd', wts.astype(outs.dtype), outs)
    return o.reshape(bsz, L, A_WIDTH) @ w_out, new_bufs


def _diff_attention(q1, q2, k1, k2, v, q_pos, k_pos, lam):
    scale = HEAD_DIM ** -0.5

    def block(s, nq):
        a1 = lax.dynamic_slice_in_dim(q1, s, nq, axis=1)
        a2 = lax.dynamic_slice_in_dim(q2, s, nq, axis=1)
        qp = lax.dynamic_slice_in_dim(q_pos, s, nq, axis=0)
        mask = (k_pos[None, :] <= qp[:, None])[None, None]
        s1 = jnp.einsum('bqhd,bkhd->bhqk', a1, k1).astype(jnp.float32) * scale
        s2 = jnp.einsum('bqhd,bkhd->bhqk', a2, k2).astype(jnp.float32) * scale
        p1 = jax.nn.softmax(jnp.where(mask, s1, -jnp.inf), axis=-1)
        p2 = jax.nn.softmax(jnp.where(mask, s2, -jnp.inf), axis=-1)
        attn = (p1 - lam * p2).astype(v.dtype)
        return jnp.einsum('bhqk,bkhd->bqhd', attn, v)

    return _map_query_blocks(block, q1.shape[1])


def _mixer_b(h, w_in, w_out, lam_p, subln_g, lam_init, pos, past):
    bsz, L, _ = h.shape
    proj = (h @ w_in).reshape(bsz, L, 3, B_HEADS, B_VDIM)
    q1 = _rope(proj[:, :, 0, :, :HEAD_DIM], pos)
    q2 = _rope(proj[:, :, 0, :, HEAD_DIM:], pos)
    k1 = _rope(proj[:, :, 1, :, :HEAD_DIM], pos)
    k2 = _rope(proj[:, :, 1, :, HEAD_DIM:], pos)
    v = proj[:, :, 2]
    kv = jnp.stack([jnp.concatenate([k1, k2], axis=-1), v], axis=2)
    lp = lam_p.astype(jnp.float32)
    lam = jnp.exp(jnp.sum(lp[0] * lp[1])) - jnp.exp(jnp.sum(lp[2] * lp[3])) + lam_init
    if past is None:
        o = _diff_attention(q1, q2, k1, k2, v, pos, pos, lam)
    else:
        pool, page_table = past
        n_past = page_table.shape[1] * PAGE_SIZE
        k_pos = jnp.arange(n_past + L, dtype=jnp.int32)

        def one_seq(args):
            a1, a2, kvn, pages = args
            kv_all = jnp.concatenate(
                [pool[pages].reshape((n_past,) + kvn.shape[1:]), kvn], axis=0)
            k_all, v_all = kv_all[None, :, 0], kv_all[None, :, 1]
            return _diff_attention(a1[None], a2[None], k_all[..., :HEAD_DIM],
                                   k_all[..., HEAD_DIM:], v_all, pos, k_pos, lam)[0]

        o = lax.map(one_seq, (q1, q2, kv, page_table))
    of = o.astype(jnp.float32)
    of = of * lax.rsqrt(jnp.mean(jnp.square(of), axis=-1, keepdims=True) + LN_EPS)
    o = (of * subln_g * (1.0 - lam_init)).astype(h.dtype)
    return o.reshape(bsz, L, B_HEADS * B_VDIM) @ w_out, kv


def _indexer_scores(qi, wi, ki, q_pos, k_pos):
    r = jax.nn.relu(jnp.einsum('bqhd,bkd->bqhk', qi, ki).astype(jnp.float32))
    sc = jnp.einsum('bqhk,bqh->bqk', r, wi.astype(jnp.float32))
    mask = (k_pos[None, :] <= q_pos[:, None])[None]
    return jnp.where(mask, sc, -jnp.inf)


def _sparse_attention(q, sel, valid):
    bsz, nq = q.shape[:2]
    qg = q.reshape(bsz, nq, C_KV_HEADS, C_HEADS // C_KV_HEADS, HEAD_DIM)
    sc = jnp.einsum('bqkgd,bqnkd->bqkgn', qg, sel[:, :, :, 0]).astype(jnp.float32) * HEAD_DIM ** -0.5
    p = jax.nn.softmax(jnp.where(valid[:, :, None, None, :], sc, -jnp.inf), axis=-1)
    o = jnp.einsum('bqkgn,bqnkd->bqkgd', p.astype(q.dtype), sel[:, :, :, 1])
    return o.reshape(bsz, nq, C_HEADS, HEAD_DIM)


def _mixer_c(h, w_in, w_out, pos, past):
    bsz, L, _ = h.shape
    q, k, v, qi, ki, wi = jnp.split(h @ w_in, list(C_SPLITS), axis=-1)
    q = _rope(q.reshape(bsz, L, C_HEADS, HEAD_DIM), pos)
    k = _rope(k.reshape(bsz, L, C_KV_HEADS, HEAD_DIM), pos)
    v = v.reshape(bsz, L, C_KV_HEADS, HEAD_DIM)
    qi = _rope(qi.reshape(bsz, L, IDX_HEADS, IDX_DIM), pos)
    ki = _rope(ki.reshape(bsz, L, 1, IDX_DIM), pos)[:, :, 0]
    wi = wi * IDX_W_SCALE
    kv = jnp.stack([k, v], axis=2)
    take = jax.vmap(lambda a, i: a[i])
    if past is None:
        topk = min(C_TOPK_MAX, L // 4)

        def block(s, nq):
            qb = lax.dynamic_slice_in_dim(q, s, nq, axis=1)
            qib = lax.dynamic_slice_in_dim(qi, s, nq, axis=1)
            wib = lax.dynamic_slice_in_dim(wi, s, nq, axis=1)
            qp = lax.dynamic_slice_in_dim(pos, s, nq, axis=0)
            vals, idx = lax.top_k(_indexer_scores(qib, wib, ki, qp, pos), topk)
            return _sparse_attention(qb, take(kv, idx), jnp.isfinite(vals))

        o = _map_query_blocks(block, L)
    else:
        pool_kv, pool_idx, page_table = past
        n_pages = page_table.shape[1]
        n_past = n_pages * PAGE_SIZE
        ki_all = jnp.concatenate([pool_idx[page_table].reshape(bsz, n_past, IDX_DIM), ki], axis=1)
        k_pos = jnp.arange(n_past + L, dtype=jnp.int32)
        topk = min(C_TOPK_MAX, (n_past + L) // 4)
        vals, idx = lax.top_k(_indexer_scores(qi, wi, ki_all, pos, k_pos), topk)
        phys = take(page_table, jnp.minimum(idx // PAGE_SIZE, n_pages - 1))
        sel_past = pool_kv[phys, idx % PAGE_SIZE]
        sel_new = take(kv, jnp.clip(idx - n_past, 0, L - 1))
        sel = jnp.where((idx < n_past)[..., None, None, None], sel_past, sel_new)
        o = _sparse_attention(q, sel, jnp.isfinite(vals))
    return o.reshape(bsz, L, C_HEADS * HEAD_DIM) @ w_out, kv, ki


def _swiglu(h, w_in, w_out):
    g, u = jnp.split(h @ w_in, 2, axis=-1)
    return (jax.nn.silu(g) * u) @ w_out


def _to_pages(a):
    return a.reshape((a.shape[0] * (a.shape[1] // PAGE_SIZE), PAGE_SIZE) + a.shape[2:])


def setup_inputs(seed: int = 0) -> dict:
    key = jax.random.key(seed)
    ks = jax.random.split(key, 32)
    d = D_MODEL
    beta = DEEPNORM_BETA

    def nrm(k, shape, scale=1.0):
        return jax.random.normal(k, shape, jnp.float32) * scale

    n_pages = PAST_LEN // PAGE_SIZE
    n_used = DEC_BATCH * n_pages
    n_phys = n_used + max(1, n_used // 4)
    page_table = jax.random.permutation(ks[8], n_phys)[:n_used].reshape(DEC_BATCH, n_pages).astype(jnp.int32)

    def a_state(k, window):
        return nrm(k, (N_A, DEC_BATCH, min(window, PAST_LEN), 2, A_HEADS, HEAD_DIM))

    return {
        'x_prompt': nrm(ks[0], (BATCH, SEQ, d)),
        'x_sample': nrm(ks[1], (DEC_BATCH, DEC_SEQ, d)),
        'state_a_kv_w128': a_state(ks[2], A_GROUPS[0][0]),
        'state_a_kv_w512': a_state(ks[3], A_GROUPS[1][0]),
        'state_a_kv_w2048': a_state(ks[4], A_GROUPS[2][0]),
        'cache_b_kv': nrm(ks[5], (N_B, n_phys, PAGE_SIZE, 2, B_HEADS, B_VDIM)),
        'cache_c_kv': nrm(ks[6], (N_C, n_phys, PAGE_SIZE, 2, C_KV_HEADS, HEAD_DIM)),
        'cache_c_idx': nrm(ks[7], (N_C, n_phys, PAGE_SIZE, IDX_DIM)),
        'page_table': page_table,
        'c_prompt': nrm(ks[9], (BATCH, d)),
        'c_sample': nrm(ks[10], (DEC_BATCH, d)),
        'w_mod': nrm(ks[11], (DEPTH, d, 6 * d), 0.5 * d ** -0.5),
        'b_mod': nrm(ks[12], (DEPTH, 6 * d), 0.02),
        'ln_g': 1.0 + nrm(ks[13], (DEPTH, 2, d), 0.02),
        'ln_b': nrm(ks[14], (DEPTH, 2, d), 0.02),
        'w_ffn_in': nrm(ks[15], (DEPTH, d, 2 * FFN_HIDDEN), d ** -0.5),
        'w_ffn_out': nrm(ks[16], (DEPTH, FFN_HIDDEN, d), beta * FFN_HIDDEN ** -0.5),
        'a_w_in': nrm(ks[17], (N_A, d, A_IN), d ** -0.5),
        'a_w_out': nrm(ks[18], (N_A, A_WIDTH, d), beta * A_WIDTH ** -0.5),
        'b_w_in': nrm(ks[19], (N_B, d, B_IN), d ** -0.5),
        'b_w_out': nrm(ks[20], (N_B, B_HEADS * B_VDIM, d), beta * (B_HEADS * B_VDIM) ** -0.5),
        'b_lambda': nrm(ks[21], (N_B, 4, HEAD_DIM), 0.1),
        'b_subln_g': 1.0 + nrm(ks[22], (N_B, B_VDIM), 0.02),
        'c_w_in': nrm(ks[23], (N_C, d, C_IN), d ** -0.5),
        'c_w_out': nrm(ks[24], (N_C, C_HEADS * HEAD_DIM, d), beta * (C_HEADS * HEAD_DIM) ** -0.5),
    }


def reference(x_prompt, x_sample, state_a_kv_w128, state_a_kv_w512, state_a_kv_w2048,
              cache_b_kv, cache_c_kv, cache_c_idx, page_table, c_prompt, c_sample,
              w_mod, b_mod, ln_g, ln_b, w_ffn_in, w_ffn_out, a_w_in, a_w_out,
              b_w_in, b_w_out, b_lambda, b_subln_g, c_w_in, c_w_out):
    a_states = (state_a_kv_w128, state_a_kv_w512, state_a_kv_w2048)

    def run(x, c, pos, sample):
        new_a = [[] for _ in A_GROUPS]
        new_b, new_ckv, new_cidx = [], [], []
        for i in range(DEPTH):
            kind, j = i % N_MIXERS, i // N_MIXERS
            mod = jax.nn.silu(c) @ w_mod[i] + b_mod[i]
            sh1, sc1, g1, sh2, sc2, g2 = jnp.split(mod[:, None, :], 6, axis=-1)
            h = x * (1.0 + sc1) + sh1
            if kind == 0:
                bufs = [s[j] for s in a_states] if sample else None
                m, nb = _mixer_a(h, a_w_in[j], a_w_out[j], pos, bufs)
                for g in range(len(A_GROUPS)):
                    new_a[g].append(nb[g])
            elif kind == 1:
                past = (cache_b_kv[j], page_table) if sample else None
                lam_init = 0.8 - 0.6 * math.exp(-0.3 * i)
                m, kv = _mixer_b(h, b_w_in[j], b_w_out[j], b_lambda[j], b_subln_g[j], lam_init, pos, past)
                new_b.append(kv if sample else _to_pages(kv))
            else:
                past = (cache_c_kv[j], cache_c_idx[j], page_table) if sample else None
                m, kv, ki = _mixer_c(h, c_w_in[j], c_w_out[j], pos, past)
                new_ckv.append(kv if sample else _to_pages(kv))
                new_cidx.append(ki if sample else _to_pages(ki))
            x = _layer_norm(DEEPNORM_ALPHA * x + (1.0 + g1) * m, ln_g[i, 0], ln_b[i, 0])
            h = x * (1.0 + sc2) + sh2
            x = _layer_norm(DEEPNORM_ALPHA * x + (1.0 + g2) * _swiglu(h, w_ffn_in[i], w_ffn_out[i]),
                            ln_g[i, 1], ln_b[i, 1])
        return (x, [jnp.stack(a) for a in new_a], jnp.stack(new_b),
                jnp.stack(new_ckv), jnp.stack(new_cidx))

    n_past = page_table.shape[1] * PAGE_SIZE
    pos_p = jnp.arange(x_prompt.shape[1], dtype=jnp.int32)
    pos_s = n_past + jnp.arange(x_sample.shape[1], dtype=jnp.int32)
    y_prompt, a_p, b_kv_p, c_kv_p, c_idx_p = run(x_prompt, c_prompt, pos_p, False)
    y_sample, a_s, b_kv_s, c_kv_s, c_idx_s = run(x_sample, c_sample, pos_s, True)
    return (y_prompt, y_sample, a_p[0], a_s[0], a_p[1], a_s[1], a_p[2], a_s[2],
            b_kv_p, b_kv_s, c_kv_p, c_kv_s, c_idx_p, c_idx_s)
```

```python
import functools
import math

import jax
import jax.numpy as jnp
from jax import lax
from jax.experimental import pallas as pl
from jax.experimental.pallas import tpu as pltpu

F32 = jnp.float32
BF16 = jnp.bfloat16

DEPTH = 4
N_MIXERS = 3
D_MODEL = 1024
HEAD_DIM = 64
PAGE_SIZE = 128
ROPE_THETA = 500000.0
LN_EPS = 1e-5
DEEPNORM_ALPHA = (2 * DEPTH) ** 0.25
QK_SCALE = HEAD_DIM ** -0.5

A_GROUPS = ((128, 1), (512, 4), (2048, 16))
A_HEADS = 8
A_WIDTH = A_HEADS * HEAD_DIM
B_HEADS = 8
B_VDIM = 2 * HEAD_DIM
C_HEADS = 16
C_KV_HEADS = 4
IDX_HEADS = 8
IDX_DIM = 64
C_TOPK_MAX = 256
IDX_W_SCALE = IDX_HEADS ** -0.5 * IDX_DIM ** -0.5
FFN_HIDDEN = 2816

LANES = 128
VMEM_LIMIT_BYTES = 56 * 1024 * 1024
NEG_INF = float("-inf")
INT_MIN = -(2 ** 31)


def _cparams(semantics, vmem=VMEM_LIMIT_BYTES):
    return pltpu.CompilerParams(dimension_semantics=semantics, vmem_limit_bytes=vmem)


def _dot(a, b):
    return jnp.dot(a, b, preferred_element_type=F32)


def _dot_nt(a, b):
    return lax.dot_general(a, b, (((1,), (1,)), ((), ())), preferred_element_type=F32)


def _resident(shape):
    nd = len(shape)
    return pl.BlockSpec(shape, lambda *_: (0,) * nd, pipeline_mode=pl.Buffered(1))


def _split_bf16(a):
    hi = a.astype(BF16)
    lo = (a - hi.astype(F32)).astype(BF16)
    return hi, lo


def _mod_kernel(c_ref, w_ref, b_ref, o_ref):
    c = c_ref[...]
    a_hi, a_lo = _split_bf16(jax.nn.silu(c))
    w_hi, w_lo = _split_bf16(w_ref[...])
    o_ref[...] = _dot(a_hi, w_hi) + _dot(a_hi, w_lo) + _dot(a_lo, w_hi) + b_ref[...]


def _modulation(c_all, w_mod, b_mod):
    n, d = c_all.shape
    depth, _, n6 = w_mod.shape
    tn = 1536
    return pl.pallas_call(
        _mod_kernel,
        out_shape=jax.ShapeDtypeStruct((depth, n, n6), F32),
        grid=(depth, n6 // tn),
        in_specs=[
            pl.BlockSpec((n, d), lambda i, j: (0, 0)),
            pl.BlockSpec((None, d, tn), lambda i, j: (i, 0, j)),
            pl.BlockSpec((None, 1, tn), lambda i, j: (i, 0, j)),
        ],
        out_specs=pl.BlockSpec((None, n, tn), lambda i, j: (i, 0, j)),
        compiler_params=_cparams(("arbitrary", "arbitrary")),
        name="adaln_modulation",
    )(c_all, w_mod, b_mod.reshape(depth, 1, n6))


def _rope_tables(pos):
    rot = HEAD_DIM // 4
    half = rot // 2
    inv = ROPE_THETA ** (-jnp.arange(0, rot, 2, dtype=F32) / rot)
    ang = pos.astype(F32)[:, None] * inv[None, :]
    cos, sin = jnp.cos(ang), jnp.sin(ang)
    n = pos.shape[0]
    ones = jnp.ones((n, HEAD_DIM - rot), F32)
    zeros = jnp.zeros((n, HEAD_DIM - rot), F32)
    zh = jnp.zeros((n, half), F32)
    c = jnp.concatenate([cos, cos, ones], axis=1)
    sa = jnp.concatenate([-sin, zh, zeros], axis=1)
    sb = jnp.concatenate([zh, sin, zeros], axis=1)
    rep = LANES // HEAD_DIM
    return jnp.tile(c, (1, rep)), jnp.tile(sa, (1, rep)), jnp.tile(sb, (1, rep))


def _inproj_kernel(x_ref, mod_ref, cos_ref, sa_ref, sb_ref, w_ref, *out_refs,
                   plan, keeps, tiles_per_seq):
    d = x_ref.shape[1]
    tm = x_ref.shape[0]
    x = x_ref[...]
    h = (x * (1.0 + mod_ref[:, d:2 * d]) + mod_ref[:, 0:d]).astype(BF16)
    cos, sa, sb = cos_ref[...], sa_ref[...], sb_ref[...]
    half = HEAD_DIM // 8
    t_in = pl.program_id(0) % tiles_per_seq
    for c0, wd, rope_cols, scale, dests in plan:
        y = _dot(h, w_ref[:, c0:c0 + wd])
        if rope_cols:
            reps = wd // LANES
            tile = lambda t: jnp.concatenate([t] * reps, axis=1) if reps > 1 else t
            yr = (y * tile(cos) + pltpu.roll(y, wd - half, 1) * tile(sa)
                  + pltpu.roll(y, half, 1) * tile(sb))
            if rope_cols < wd:
                lane = lax.broadcasted_iota(jnp.int32, y.shape, 1)
                y = jnp.where(lane < rope_cols, yr, y)
            else:
                y = yr
        if scale != 1.0:
            y = y * scale
        for oi, col in dests:
            o_ref = out_refs[oi]
            keep = keeps[oi]
            if keep is None:
                o_ref[:, col:col + wd] = y.astype(o_ref.dtype)
            elif keep >= tm:
                first = tiles_per_seq - keep // tm

                @pl.when(t_in >= first)
                def _(o_ref=o_ref, y=y, col=col, wd=wd):
                    o_ref[:, col:col + wd] = y.astype(o_ref.dtype)
            else:
                @pl.when(t_in == tiles_per_seq - 1)
                def _(o_ref=o_ref, y=y, col=col, wd=wd, keep=keep):
                    o_ref[:, col:col + wd] = y[tm - keep:, :].astype(o_ref.dtype)


def _inproj(x, mod3, tables, w, plan, outs, tm, seq_len):
    r, d = x.shape
    n_tiles = r // tm
    tps = seq_len // tm if mod3.shape[1] == 1 else 1
    tbl_tiles = tables[0].shape[0] // tm
    rm = mod3.shape[1]
    out_shapes, out_specs, keeps = [], [], []
    for cols, dtype, keep in outs:
        keeps.append(keep)
        if keep is None:
            out_shapes.append(jax.ShapeDtypeStruct((r, cols), dtype))
            out_specs.append(pl.BlockSpec((tm, cols), lambda t: (t, 0)))
        elif keep >= tm:
            assert keep % tm == 0 and mod3.shape[1] == 1
            first = tps - keep // tm
            out_shapes.append(jax.ShapeDtypeStruct((r // seq_len, keep, cols), dtype))
            out_specs.append(pl.BlockSpec(
                (None, tm, cols),
                lambda t, first=first: (t // tps, jnp.maximum(t % tps - first, 0), 0)))
        else:
            assert tm % keep == 0 and keep % 8 == 0 and mod3.shape[1] == 1
            out_shapes.append(jax.ShapeDtypeStruct((r // seq_len, keep, cols), dtype))
            out_specs.append(pl.BlockSpec((None, keep, cols), lambda t: (t // tps, 0, 0)))
    tbl_spec = pl.BlockSpec((tm, LANES), lambda t: (t % tbl_tiles, 0))
    kernel = functools.partial(_inproj_kernel, plan=tuple(plan), keeps=tuple(keeps),
                               tiles_per_seq=tps)
    return pl.pallas_call(
        kernel,
        out_shape=out_shapes,
        grid=(n_tiles,),
        in_specs=[
            pl.BlockSpec((tm, d), lambda t: (t, 0)),
            pl.BlockSpec((None, rm, 2 * d), lambda t: (t // tps, 0, 0)),
            tbl_spec, tbl_spec, tbl_spec,
            _resident(w.shape),
        ],
        out_specs=out_specs,
        compiler_params=_cparams(("arbitrary",)),
        name="mixer_in_proj",
    )(x, mod3, *tables, w)


def _layer_norm(x, g, b):
    mu = jnp.mean(x, axis=-1, keepdims=True)
    xc = x - mu
    var = jnp.mean(xc * xc, axis=-1, keepdims=True)
    return xc * lax.rsqrt(var + LN_EPS) * g + b


FFN_CHUNK = 256


def _post_kernel(x_ref, o_ref, mod_ref, wo_ref, lng_ref, lnb_ref, w1_ref, w2_ref,
                 y_ref, acc_ref):
    d = x_ref.shape[1]
    f = w2_ref.shape[0]
    x = x_ref[...]
    g1 = mod_ref[:, 2 * d:3 * d]
    sh2 = mod_ref[:, 3 * d:4 * d]
    sc2 = mod_ref[:, 4 * d:5 * d]
    g2 = mod_ref[:, 5 * d:6 * d]
    m = _dot(o_ref[...], wo_ref[...])
    x1 = _layer_norm(DEEPNORM_ALPHA * x + (1.0 + g1) * m, lng_ref[0:1, :], lnb_ref[0:1, :])
    h = (x1 * (1.0 + sc2) + sh2).astype(BF16)
    for c in range(f // FFN_CHUNK):
        c0 = c * FFN_CHUNK
        g = _dot(h, w1_ref[:, c0:c0 + FFN_CHUNK])
        u = _dot(h, w1_ref[:, f + c0:f + c0 + FFN_CHUNK])
        part = _dot((jax.nn.silu(g) * u).astype(BF16), w2_ref[c0:c0 + FFN_CHUNK, :])
        if c == 0:
            acc_ref[...] = part
        else:
            acc_ref[...] += part
    y_ref[...] = _layer_norm(DEEPNORM_ALPHA * x1 + (1.0 + g2) * acc_ref[...],
                             lng_ref[1:2, :], lnb_ref[1:2, :])


def _post(x, o, mod3, wo, lng, lnb, w1, w2, tm, seq_len):
    r, d = x.shape
    wo_in = o.shape[1]
    rm = mod3.shape[1]
    tps = seq_len // tm if rm == 1 else 1
    return pl.pallas_call(
        _post_kernel,
        out_shape=jax.ShapeDtypeStruct((r, d), F32),
        grid=(r // tm,),
        in_specs=[
            pl.BlockSpec((tm, d), lambda t: (t, 0)),
            pl.BlockSpec((tm, wo_in), lambda t: (t, 0)),
            pl.BlockSpec((None, rm, 6 * d), lambda t: (t // tps, 0, 0)),
            _resident(wo.shape), _resident(lng.shape), _resident(lnb.shape),
            _resident(w1.shape), _resident(w2.shape),
        ],
        out_specs=pl.BlockSpec((tm, d), lambda t: (t, 0)),
        scratch_shapes=[pltpu.VMEM((tm, d), F32)],
        compiler_params=_cparams(("arbitrary",)),
        name="out_proj_ffn",
    )(x, o, mod3, wo, lng, lnb, w1, w2)


def _pair_stack(qp):
    lane = lax.broadcasted_iota(jnp.int32, qp.shape, 1)
    lo = lane < HEAD_DIM
    zero = jnp.zeros_like(qp)
    return jnp.concatenate([jnp.where(lo, qp, zero), jnp.where(lo, zero, qp)], axis=0)


def _pair_merge(top, bot):
    lane = lax.broadcasted_iota(jnp.int32, top.shape, 1)
    return jnp.where(lane < HEAD_DIM, top, bot)


def _attn_a_prompt_kernel(q_ref, kvc_ref, kvp_ref, o_ref, lse_ref):
    tq = q_ref.shape[0]
    width = q_ref.shape[1]
    i = pl.program_id(2)
    row = lax.broadcasted_iota(jnp.int32, (2 * tq, tq), 0) % tq
    col = lax.broadcasted_iota(jnp.int32, (2 * tq, tq), 1)
    mask_c = col <= row
    mask_p = jnp.logical_and(col >= row, i > 0)
    for j in range(width // LANES):
        sl = slice(j * LANES, (j + 1) * LANES)
        vsl = slice(width + j * LANES, width + (j + 1) * LANES)
        qs = _pair_stack(q_ref[:, sl])
        s_c = jnp.where(mask_c, _dot_nt(qs, kvc_ref[:, sl]), NEG_INF)
        s_p = jnp.where(mask_p, _dot_nt(qs, kvp_ref[:, sl]), NEG_INF)
        m = jnp.maximum(jnp.max(s_c, axis=1, keepdims=True), jnp.max(s_p, axis=1, keepdims=True))
        p_c = jnp.exp(s_c - m)
        p_p = jnp.exp(s_p - m)
        l = jnp.sum(p_c, axis=1, keepdims=True) + jnp.sum(p_p, axis=1, keepdims=True)
        ov = _dot(p_c.astype(BF16), kvc_ref[:, vsl]) + _dot(p_p.astype(BF16), kvp_ref[:, vsl])
        ov = ov / l
        lse = jnp.broadcast_to(m + jnp.log(l), ov.shape)
        o_ref[:, sl] = _pair_merge(ov[:tq], ov[tq:])
        lse_ref[:, sl] = _pair_merge(lse[:tq], lse[tq:])


def _attn_a_prompt(q, kv):
    b, dil, ld, width = q.shape
    tq = PAGE_SIZE
    nb = ld // tq
    blk = lambda w: (None, None, tq, w)
    o, lse = pl.pallas_call(
        _attn_a_prompt_kernel,
        out_shape=[jax.ShapeDtypeStruct(q.shape, F32), jax.ShapeDtypeStruct(q.shape, F32)],
        grid=(b, dil, nb),
        in_specs=[
            pl.BlockSpec(blk(width), lambda b_, r, i: (b_, r, i, 0)),
            pl.BlockSpec(blk(2 * width), lambda b_, r, i: (b_, r, i, 0)),
            pl.BlockSpec(blk(2 * width), lambda b_, r, i: (b_, r, jnp.maximum(i - 1, 0), 0)),
        ],
        out_specs=[pl.BlockSpec(blk(width), lambda b_, r, i: (b_, r, i, 0))] * 2,
        compiler_params=_cparams(("arbitrary",) * 3),
        name="dilated_attn_prompt",
    )(q, kv, kv)
    return o, lse


def _attn_a_sample_kernel(q_ref, st_ref, kvn_ref, o_ref, lse_ref, *, dil):
    nq, width = q_ref.shape
    w = st_ref.shape[0]
    pad = LANES - nq
    jq = lax.broadcasted_iota(jnp.int32, (2 * nq, w), 0) % nq
    ik = lax.broadcasted_iota(jnp.int32, (2 * nq, w), 1)
    off = w + jq - ik
    mask_b = jnp.logical_and(off <= w, (off & (dil - 1)) == 0)
    jq_n = lax.broadcasted_iota(jnp.int32, (2 * nq, LANES), 0) % nq
    jn = lax.broadcasted_iota(jnp.int32, (2 * nq, LANES), 1)
    off_n = jq_n - jn
    mask_n = jnp.logical_and(jnp.logical_and(off_n >= 0, jn < nq), (off_n & (dil - 1)) == 0)
    for j in range(width // LANES):
        sl = slice(j * LANES, (j + 1) * LANES)
        vsl = slice(width + j * LANES, width + (j + 1) * LANES)
        qs = _pair_stack(q_ref[:, sl]).astype(BF16)
        kb = st_ref[:, sl].astype(BF16)
        vb = st_ref[:, vsl].astype(BF16)
        zpad = jnp.zeros((pad, LANES), F32)
        kn = jnp.concatenate([kvn_ref[:, sl], zpad], axis=0).astype(BF16)
        vn = jnp.concatenate([kvn_ref[:, vsl], zpad], axis=0).astype(BF16)
        s_b = jnp.where(mask_b, _dot_nt(qs, kb), NEG_INF)
        s_n = jnp.where(mask_n, _dot_nt(qs, kn), NEG_INF)
        m = jnp.maximum(jnp.max(s_b, axis=1, keepdims=True), jnp.max(s_n, axis=1, keepdims=True))
        p_b = jnp.exp(s_b - m)
        p_n = jnp.exp(s_n - m)
        l = jnp.sum(p_b, axis=1, keepdims=True) + jnp.sum(p_n, axis=1, keepdims=True)
        ov = (_dot(p_b.astype(BF16), vb) + _dot(p_n.astype(BF16), vn)) / l
        lse = jnp.broadcast_to(m + jnp.log(l), ov.shape)
        o_ref[:, sl] = _pair_merge(ov[:nq], ov[nq:])
        lse_ref[:, sl] = _pair_merge(lse[:nq], lse[nq:])


def _attn_a_sample(q, state, kvn, dil):
    s, nq, width = q.shape
    w = state.shape[1]
    return pl.pallas_call(
        functools.partial(_attn_a_sample_kernel, dil=dil),
        out_shape=[jax.ShapeDtypeStruct(q.shape, F32)] * 2,
        grid=(s,),
        in_specs=[
            pl.BlockSpec((None, nq, width), lambda b: (b, 0, 0)),
            pl.BlockSpec((None, w, 2 * width), lambda b: (b, 0, 0)),
            pl.BlockSpec((None, nq, 2 * width), lambda b: (b, 0, 0)),
        ],
        out_specs=[pl.BlockSpec((None, nq, width), lambda b: (b, 0, 0))] * 2,
        compiler_params=_cparams(("arbitrary",)),
        name="dilated_attn_sample",
    )(q, state, kvn)


def _combine_a_kernel(o0, o1, o2, l0, l1, l2, out_ref):
    m = jnp.maximum(jnp.maximum(l0[...], l1[...]), l2[...])
    w0 = jnp.exp(l0[...] - m)
    w1 = jnp.exp(l1[...] - m)
    w2 = jnp.exp(l2[...] - m)
    num = w0 * o0[...] + w1 * o1[...] + w2 * o2[...]
    out_ref[...] = (num / (w0 + w1 + w2)).astype(out_ref.dtype)


def _combine_a(os_, lses, tm):
    r, width = os_[0].shape
    spec = pl.BlockSpec((tm, width), lambda t: (t, 0))
    return pl.pallas_call(
        _combine_a_kernel,
        out_shape=jax.ShapeDtypeStruct((r, width), BF16),
        grid=(r // tm,),
        in_specs=[spec] * 6,
        out_specs=spec,
        compiler_params=_cparams(("arbitrary",)),
        name="dilated_combine",
    )(*os_, *lses)


def _plan_a():
    plan = []
    for g in range(len(A_GROUPS)):
        base = g * 3 * A_WIDTH
        plan.append((base, A_WIDTH, A_WIDTH, QK_SCALE, ((g, 0),)))
        plan.append((base + A_WIDTH, A_WIDTH, A_WIDTH, 1.0, ((3 + g, 0), (6 + g, 0))))
        plan.append((base + 2 * A_WIDTH, A_WIDTH, 0, 1.0, ((3 + g, A_WIDTH), (6 + g, A_WIDTH))))
    return plan


def _mixer_a_prompt(x, mod3, tables, w_in, bsz, seq_len, tm):
    keeps = [min(win, seq_len) for win, _ in A_GROUPS]
    outs = ([(A_WIDTH, BF16, None)] * 3 + [(2 * A_WIDTH, BF16, None)] * 3
            + [(2 * A_WIDTH, F32, k) for k in keeps])
    res = _inproj(x, mod3, tables, w_in, _plan_a(), outs, tm, seq_len)
    os_, lses = [], []
    for g, (_, dil) in enumerate(A_GROUPS):
        ld = seq_len // dil
        perm = lambda a: a.reshape(bsz, ld, dil, a.shape[-1]).transpose(0, 2, 1, 3)
        o, lse = _attn_a_prompt(perm(res[g]), perm(res[3 + g]))
        unperm = lambda a: a.transpose(0, 2, 1, 3).reshape(bsz * seq_len, A_WIDTH)
        os_.append(unperm(o))
        lses.append(unperm(lse))
    o = _combine_a(os_, lses, tm)
    bufs = [res[6 + g].reshape(bsz, keeps[g], 2, A_HEADS, HEAD_DIM) for g in range(3)]
    return o, bufs


def _mixer_a_sample(x, mod3, tables, w_in, states, n_seq, n_new):
    outs = [(A_WIDTH, F32, None)] * 3 + [(2 * A_WIDTH, F32, None)] * 3 + [(2 * A_WIDTH, F32, None)] * 3
    res = _inproj(x, mod3, tables, w_in, _plan_a(), outs, x.shape[0], n_new)
    os_, lses, bufs = [], [], []
    for g, (_, dil) in enumerate(A_GROUPS):
        st = states[g]
        w = st.shape[1]
        st2 = st.reshape(n_seq, w, 2 * A_WIDTH)
        kvn = res[3 + g].reshape(n_seq, n_new, 2 * A_WIDTH)
        o, lse = _attn_a_sample(res[g].reshape(n_seq, n_new, A_WIDTH), st2, kvn, dil)
        os_.append(o.reshape(n_seq * n_new, A_WIDTH))
        lses.append(lse.reshape(n_seq * n_new, A_WIDTH))
        full = jnp.concatenate([st2, kvn], axis=1)
        bufs.append(full[:, full.shape[1] - w:].reshape(n_seq, w, 2, A_HEADS, HEAD_DIM))
    o = _combine_a(os_, lses, x.shape[0])
    return o, bufs


def _lambda_of(lam_ref, lam_init):
    lp = lam_ref[...]
    a = jnp.sum(lp[0:1, :] * lp[1:2, :], axis=1, keepdims=True)
    b = jnp.sum(lp[2:3, :] * lp[3:4, :], axis=1, keepdims=True)
    return jnp.exp(a) - jnp.exp(b) + lam_init


def _sub_ln(of, g, lam_init):
    ms = jnp.mean(of * of, axis=-1, keepdims=True)
    return of * lax.rsqrt(ms + LN_EPS) * g * (1.0 - lam_init)


def _attn_b_prompt_kernel(q_ref, k_ref, v_ref, lam_ref, g_ref, o_ref, m_sc, l_sc, acc_sc,
                          *, lam_init):
    tq = q_ref.shape[0]
    tk = k_ref.shape[0]
    qi = pl.program_id(2)
    ki = pl.program_id(3)

    @pl.when(ki == 0)
    def _():
        m_sc[...] = jnp.full(m_sc.shape, NEG_INF, F32)
        l_sc[...] = jnp.zeros(l_sc.shape, F32)
        acc_sc[...] = jnp.zeros(acc_sc.shape, F32)

    @pl.when(ki <= qi)
    def _():
        qs = _pair_stack(q_ref[...])
        s = _dot_nt(qs, k_ref[...])
        row = lax.broadcasted_iota(jnp.int32, s.shape, 0) % tq + qi * tq
        col = lax.broadcasted_iota(jnp.int32, s.shape, 1) + ki * tk
        s = jnp.where(col <= row, s, NEG_INF)
        m_old = m_sc[...]
        m_new = jnp.maximum(m_old, jnp.max(s, axis=1, keepdims=True))
        alpha = jnp.exp(m_old - m_new)
        p = jnp.exp(s - m_new)
        l_sc[...] = alpha * l_sc[...] + jnp.sum(p, axis=1, keepdims=True)
        acc_sc[...] = alpha * acc_sc[...] + _dot(p.astype(BF16), v_ref[...])
        m_sc[...] = m_new

    @pl.when(ki == pl.num_programs(3) - 1)
    def _():
        lam = _lambda_of(lam_ref, lam_init)
        o = acc_sc[...] / l_sc[...]
        of = o[:tq] - lam * o[tq:]
        o_ref[...] = _sub_ln(of, g_ref[...], lam_init).astype(o_ref.dtype)


def _attn_b_prompt(q, kv, lam_p, subln_g, lam_init, tq):
    b, l, hw = q.shape
    nh = hw // B_VDIM
    nq = l // tq
    kernel = functools.partial(_attn_b_prompt_kernel, lam_init=lam_init)
    return pl.pallas_call(
        kernel,
        out_shape=jax.ShapeDtypeStruct(q.shape, BF16),
        grid=(b, nh, nq, nq),
        in_specs=[
            pl.BlockSpec((None, tq, B_VDIM), lambda b_, h, i, k: (b_, i, h)),
            pl.BlockSpec((None, tq, B_VDIM), lambda b_, h, i, k: (b_, jnp.minimum(k, i), h)),
            pl.BlockSpec((None, tq, B_VDIM), lambda b_, h, i, k: (b_, jnp.minimum(k, i), nh + h)),
            pl.BlockSpec((4, HEAD_DIM), lambda *_: (0, 0)),
            pl.BlockSpec((1, B_VDIM), lambda *_: (0, 0)),
        ],
        out_specs=pl.BlockSpec((None, tq, B_VDIM), lambda b_, h, i, k: (b_, i, h)),
        scratch_shapes=[pltpu.VMEM((2 * tq, 1), F32), pltpu.VMEM((2 * tq, 1), F32),
                        pltpu.VMEM((2 * tq, B_VDIM), F32)],
        compiler_params=_cparams(("arbitrary",) * 4),
        name="diff_attn_prompt",
    )(q, kv, kv, lam_p, subln_g.reshape(1, B_VDIM))


def _attn_b_sample_kernel(pt_ref, wq_ref, page_ref, kvn_ref, lam_ref, g_ref, o_ref,
                          m_sc, l_sc, acc_sc, *, lam_init, n_new):
    hw = wq_ref.shape[1]
    rows = wq_ref.shape[0]
    p_id = pl.program_id(1)
    n_pages = pl.num_programs(1) - 1

    @pl.when(p_id == 0)
    def _():
        m_sc[...] = jnp.full(m_sc.shape, NEG_INF, F32)
        l_sc[...] = jnp.zeros(l_sc.shape, F32)
        acc_sc[...] = jnp.zeros(acc_sc.shape, F32)

    def update(s, v):
        m_old = m_sc[...]
        m_new = jnp.maximum(m_old, jnp.max(s, axis=1, keepdims=True))
        alpha = jnp.exp(m_old - m_new)
        p = jnp.exp(s - m_new)
        l_sc[...] = alpha * l_sc[...] + jnp.sum(p, axis=1, keepdims=True)
        acc_sc[...] = alpha * acc_sc[...] + _dot(p.astype(BF16), v)
        m_sc[...] = m_new

    @pl.when(p_id < n_pages)
    def _():
        k = page_ref[:, 0:hw].astype(BF16)
        v = page_ref[:, hw:2 * hw].astype(BF16)
        update(_dot_nt(wq_ref[...], k), v)

    @pl.when(p_id == n_pages)
    def _():
        pad = jnp.zeros((PAGE_SIZE - n_new, hw), F32)
        k = jnp.concatenate([kvn_ref[:, 0:hw], pad], axis=0).astype(BF16)
        v = jnp.concatenate([kvn_ref[:, hw:2 * hw], pad], axis=0).astype(BF16)
        s = _dot_nt(wq_ref[...], k)
        jq = lax.broadcasted_iota(jnp.int32, s.shape, 0) % n_new
        jk = lax.broadcasted_iota(jnp.int32, s.shape, 1)
        update(jnp.where(jk <= jq, s, NEG_INF), v)
        lam = _lambda_of(lam_ref, lam_init)
        o = acc_sc[...] / l_sc[...]
        for h in range(hw // B_VDIM):
            r0 = h * 2 * n_new
            sl = slice(h * B_VDIM, (h + 1) * B_VDIM)
            of = o[r0:r0 + n_new, sl] - lam * o[r0 + n_new:r0 + 2 * n_new, sl]
            o_ref[:, sl] = _sub_ln(of, g_ref[...], lam_init).astype(o_ref.dtype)
    del rows


def _attn_b_sample(wq, pool, kvn, page_table, lam_p, subln_g, lam_init):
    s, rows, hw = wq.shape
    n_new = kvn.shape[1]
    n_pages = page_table.shape[1]
    kernel = functools.partial(_attn_b_sample_kernel, lam_init=lam_init, n_new=n_new)
    grid_spec = pltpu.PrefetchScalarGridSpec(
        num_scalar_prefetch=1,
        grid=(s, n_pages + 1),
        in_specs=[
            pl.BlockSpec((None, rows, hw), lambda b, p, pt: (b, 0, 0)),
            pl.BlockSpec((None, PAGE_SIZE, 2 * hw),
                         lambda b, p, pt: (pt[b * n_pages + jnp.minimum(p, n_pages - 1)], 0, 0)),
            pl.BlockSpec((None, n_new, 2 * hw), lambda b, p, pt: (b, 0, 0)),
            pl.BlockSpec((4, HEAD_DIM), lambda *_: (0, 0)),
            pl.BlockSpec((1, B_VDIM), lambda *_: (0, 0)),
        ],
        out_specs=pl.BlockSpec((None, n_new, hw), lambda b, p, pt: (b, 0, 0)),
        scratch_shapes=[pltpu.VMEM((rows, 1), F32), pltpu.VMEM((rows, 1), F32),
                        pltpu.VMEM((rows, hw), F32)],
    )
    return pl.pallas_call(
        kernel,
        out_shape=jax.ShapeDtypeStruct((s, n_new, hw), BF16),
        grid_spec=grid_spec,
        compiler_params=_cparams(("arbitrary", "arbitrary")),
        name="diff_attn_sample",
    )(page_table.reshape(-1), wq, pool, kvn, lam_p, subln_g.reshape(1, B_VDIM))


def _plan_b():
    hw = B_HEADS * B_VDIM
    half = hw // 2
    return [
        (0, half, half, QK_SCALE, ((0, 0),)),
        (half, half, half, QK_SCALE, ((0, half),)),
        (hw, half, half, 1.0, ((1, 0), (2, 0))),
        (hw + half, half, half, 1.0, ((1, half), (2, half))),
        (2 * hw, half, 0, 1.0, ((1, hw), (2, hw))),
        (2 * hw + half, half, 0, 1.0, ((1, hw + half), (2, hw + half))),
    ]


def _mixer_b_prompt(x, mod3, tables, w_in, lam_p, subln_g, lam_init, bsz, seq_len, tm):
    hw = B_HEADS * B_VDIM
    outs = [(hw, BF16, None), (2 * hw, BF16, None), (2 * hw, F32, None)]
    q, kvb, kvf = _inproj(x, mod3, tables, w_in, _plan_b(), outs, tm, seq_len)
    o = _attn_b_prompt(q.reshape(bsz, seq_len, hw), kvb.reshape(bsz, seq_len, 2 * hw),
                       lam_p, subln_g, lam_init, min(512, seq_len))
    kv = kvf.reshape(bsz * seq_len // PAGE_SIZE, PAGE_SIZE, 2, B_HEADS, B_VDIM)
    return o.reshape(bsz * seq_len, hw), kv


def _mixer_b_sample(x, mod3, tables, w_in, lam_p, subln_g, lam_init, pool, page_table,
                    n_seq, n_new):
    hw = B_HEADS * B_VDIM
    outs = [(hw, BF16, None), (2 * hw, F32, None), (2 * hw, F32, None)]
    q, _, kvf = _inproj(x, mod3, tables, w_in, _plan_b(), outs, x.shape[0], n_new)
    q4 = q.reshape(n_seq, n_new, B_HEADS, 2, HEAD_DIM)
    eye_h = jnp.eye(B_HEADS, dtype=BF16)
    eye_2 = jnp.eye(2, dtype=BF16)
    wq = (q4.transpose(0, 2, 3, 1, 4)[:, :, :, :, None, None, :]
          * eye_h[None, :, None, None, :, None, None] * eye_2[None, None, :, None, None, :, None])
    wq = wq.reshape(n_seq, B_HEADS * 2 * n_new, hw)
    kvn = kvf.reshape(n_seq, n_new, 2 * hw)
    o = _attn_b_sample(wq, pool.reshape(pool.shape[0], PAGE_SIZE, 2 * hw), kvn, page_table,
                       lam_p, subln_g, lam_init)
    return o.reshape(n_seq * n_new, hw), kvf.reshape(n_seq, n_new, 2, B_HEADS, B_VDIM)


def _sortable_key(score):
    bits = lax.bitcast_convert_type(score + 0.0, jnp.int32)
    return bits ^ ((bits >> 31) & jnp.int32(0x7FFFFFFF))


def _kth_largest_key(count_ge, shape, k):
    t0 = jnp.full(shape, INT_MIN, jnp.int32)
    t0 = jnp.where(count_ge(jnp.zeros(shape, jnp.int32)) >= k, jnp.zeros(shape, jnp.int32), t0)

    def body(it, t):
        cand = t + jnp.left_shift(jnp.int32(1), 30 - it)
        return jnp.where(count_ge(cand) >= k, cand, t)

    return lax.fori_loop(0, 31, body, t0)


def _select_prompt_kernel(qi_ref, kiw_all_ref, kiw_q_ref, sel_ref, sc_ref, *, topk):
    tq = qi_ref.shape[0]
    l = kiw_all_ref.shape[0]
    i = pl.program_id(1)
    tk = min(1024, l)
    n_pairs = qi_ref.shape[1] // LANES
    wi = kiw_q_ref[:, IDX_DIM:IDX_DIM + IDX_HEADS] * IDX_W_SCALE
    qs = jnp.concatenate([_pair_stack(qi_ref[:, p * LANES:(p + 1) * LANES])
                          for p in range(n_pairs)], axis=0)
    for c in range(l // tk):
        kic = kiw_all_ref[c * tk:(c + 1) * tk, :]
        lane = lax.broadcasted_iota(jnp.int32, kic.shape, 1)
        ki2 = jnp.where(lane < IDX_DIM, kic, pltpu.roll(kic, IDX_DIM, 1)).astype(BF16)
        r = jnp.maximum(_dot_nt(qs, ki2), 0.0)
        sc = jnp.zeros((tq, tk), F32)
        for h in range(IDX_HEADS):
            sc = sc + r[h * tq:(h + 1) * tq, :] * wi[:, h:h + 1]
        row = lax.broadcasted_iota(jnp.int32, sc.shape, 0) + i * tq
        col = lax.broadcasted_iota(jnp.int32, sc.shape, 1) + c * tk
        sc_ref[:, c * tk:(c + 1) * tk] = _sortable_key(jnp.where(col <= row, sc, NEG_INF))

    def count_ge(t):
        return jnp.sum((sc_ref[...] >= t).astype(jnp.int32), axis=1, keepdims=True)

    thr = _kth_largest_key(count_ge, (tq, 1), topk)
    n_gt = jnp.sum((sc_ref[...] > thr).astype(jnp.int32), axis=1, keepdims=True)
    need = (topk - n_gt).astype(F32)
    tri = (lax.broadcasted_iota(jnp.int32, (LANES, LANES), 0)
           < lax.broadcasted_iota(jnp.int32, (LANES, LANES), 1)).astype(BF16)
    offs = jnp.zeros((tq, 1), F32)
    row = lax.broadcasted_iota(jnp.int32, (tq, LANES), 0) + i * tq
    lane = lax.broadcasted_iota(jnp.int32, (tq, LANES), 1)
    for c in range(l // LANES):
        keys = sc_ref[:, c * LANES:(c + 1) * LANES]
        causal = lane + c * LANES <= row
        eq = jnp.logical_and(keys == thr, causal)
        e = jnp.where(eq, 1.0, 0.0)
        rank = _dot(e.astype(BF16), tri) + offs
        sel = jnp.logical_or(keys > thr, jnp.logical_and(eq, rank < need))
        sel_ref[:, c * LANES:(c + 1) * LANES] = jnp.where(sel, 1.0, 0.0).astype(sel_ref.dtype)
        offs = offs + jnp.sum(e, axis=1, keepdims=True)


def _select_prompt(qi, kiw, topk):
    b, l, w = qi.shape
    tq = PAGE_SIZE
    return pl.pallas_call(
        functools.partial(_select_prompt_kernel, topk=topk),
        out_shape=jax.ShapeDtypeStruct((b, l, l), BF16),
        grid=(b, l // tq),
        in_specs=[
            pl.BlockSpec((None, tq, w), lambda b_, i: (b_, i, 0)),
            pl.BlockSpec((None, l, LANES), lambda b_, i: (b_, 0, 0)),
            pl.BlockSpec((None, tq, LANES), lambda b_, i: (b_, i, 0)),
        ],
        out_specs=pl.BlockSpec((None, tq, l), lambda b_, i: (b_, i, 0)),
        scratch_shapes=[pltpu.VMEM((tq, l), jnp.int32)],
        compiler_params=_cparams(("arbitrary", "arbitrary")),
        name="indexer_select_prompt",
    )(qi, kiw, kiw)


def _attn_c_prompt_kernel(q_ref, k_ref, v_ref, sel_ref, o_ref, m_sc, l_sc, acc_sc):
    tq = q_ref.shape[0]
    n_q_chunks = q_ref.shape[1] // LANES
    per_kv = n_q_chunks // (k_ref.shape[1] // LANES)
    qi = pl.program_id(1)
    ki = pl.program_id(2)
    tk = k_ref.shape[0]
    last = (qi * tq + tq - 1) // tk

    @pl.when(ki == 0)
    def _():
        m_sc[...] = jnp.full(m_sc.shape, NEG_INF, F32)
        l_sc[...] = jnp.zeros(l_sc.shape, F32)
        acc_sc[...] = jnp.zeros(acc_sc.shape, F32)

    @pl.when(ki <= last)
    def _():
        sel = sel_ref[...].astype(F32)
        rows_kv = 2 * per_kv * tq
        selx = jnp.concatenate([sel] * (2 * per_kv), axis=0) > 0.5
        for a in range(k_ref.shape[1] // LANES):
            qs = jnp.concatenate(
                [_pair_stack(q_ref[:, (a * per_kv + e) * LANES:(a * per_kv + e + 1) * LANES])
                 for e in range(per_kv)], axis=0)
            s = _dot_nt(qs, k_ref[:, a * LANES:(a + 1) * LANES])
            s = jnp.where(selx, s, NEG_INF)
            rs = slice(a * rows_kv, (a + 1) * rows_kv)
            m_old = m_sc[rs, :]
            m_new = jnp.maximum(m_old, jnp.max(s, axis=1, keepdims=True))
            m_use = jnp.where(m_new == NEG_INF, 0.0, m_new)
            alpha = jnp.exp(m_old - m_use)
            p = jnp.exp(s - m_use)
            l_sc[rs, :] = alpha * l_sc[rs, :] + jnp.sum(p, axis=1, keepdims=True)
            acc_sc[rs, :] = alpha * acc_sc[rs, :] + _dot(p.astype(BF16),
                                                         v_ref[:, a * LANES:(a + 1) * LANES])
            m_sc[rs, :] = m_new

    @pl.when(ki == pl.num_programs(2) - 1)
    def _():
        o = acc_sc[...] / l_sc[...]
        for c in range(n_q_chunks):
            top = o[(2 * c) * tq:(2 * c + 1) * tq, :]
            bot = o[(2 * c + 1) * tq:(2 * c + 2) * tq, :]
            o_ref[:, c * LANES:(c + 1) * LANES] = _pair_merge(top, bot).astype(o_ref.dtype)


def _attn_c_prompt(q, kd, vd, sel, tk):
    b, l, qw = q.shape
    tq = PAGE_SIZE
    kw = kd.shape[2]
    nk = l // tk
    rows = 2 * (qw // LANES) * tq
    last = lambda i: (i * tq + tq - 1) // tk
    return pl.pallas_call(
        _attn_c_prompt_kernel,
        out_shape=jax.ShapeDtypeStruct(q.shape, BF16),
        grid=(b, l // tq, nk),
        in_specs=[
            pl.BlockSpec((None, tq, qw), lambda b_, i, k: (b_, i, 0)),
            pl.BlockSpec((None, tk, kw), lambda b_, i, k: (b_, jnp.minimum(k, last(i)), 0)),
            pl.BlockSpec((None, tk, kw), lambda b_, i, k: (b_, jnp.minimum(k, last(i)), 0)),
            pl.BlockSpec((None, tq, tk), lambda b_, i, k: (b_, i, jnp.minimum(k, last(i)))),
        ],
        out_specs=pl.BlockSpec((None, tq, qw), lambda b_, i, k: (b_, i, 0)),
        scratch_shapes=[pltpu.VMEM((rows, 1), F32), pltpu.VMEM((rows, 1), F32),
                        pltpu.VMEM((rows, LANES), F32)],
        compiler_params=_cparams(("arbitrary",) * 3),
        name="sparse_attn_prompt",
    )(q, kd, vd, sel)


def _select_sample_kernel(pt_ref, qs_ref, wi_ref, page_ref, kin_ref, sel_ref, key_sc,
                          *, topk, n_new):
    p_id = pl.program_id(1)
    n_pages = pl.num_programs(1) - 1
    wi = wi_ref[...] * IDX_W_SCALE

    def scores(ki):
        r = jnp.maximum(_dot_nt(qs_ref[...], ki), 0.0)
        sc = jnp.zeros((n_new, PAGE_SIZE), F32)
        for h in range(IDX_HEADS):
            sc = sc + r[h * n_new:(h + 1) * n_new, :] * wi[:, h:h + 1]
        return sc

    @pl.when(p_id < n_pages)
    def _():
        key_sc[p_id] = _sortable_key(scores(page_ref[...].astype(BF16)))

    @pl.when(p_id == n_pages)
    def _():
        pad = jnp.zeros((PAGE_SIZE - n_new, IDX_DIM), F32)
        sc = scores(jnp.concatenate([kin_ref[...], pad], axis=0).astype(BF16))
        jq = lax.broadcasted_iota(jnp.int32, sc.shape, 0)
        jk = lax.broadcasted_iota(jnp.int32, sc.shape, 1)
        key_sc[n_pages] = _sortable_key(jnp.where(jk <= jq, sc, NEG_INF))

        def count(pred):
            c = jnp.where(pred(key_sc[...]), 1, 0)
            return jnp.sum(jnp.sum(c, axis=0), axis=1, keepdims=True)

        thr = _kth_largest_key(lambda t: count(lambda k: k >= t[None]), (n_new, 1), topk)
        need = (topk - count(lambda k: k > thr[None])).astype(F32)
        tri = (lax.broadcasted_iota(jnp.int32, (LANES, LANES), 0)
               < lax.broadcasted_iota(jnp.int32, (LANES, LANES), 1)).astype(BF16)

        def body(p, offs):
            keys = key_sc[p]
            valid = jnp.logical_or(p < n_pages, jk <= jq)
            eq = jnp.logical_and(keys == thr, valid)
            e = jnp.where(eq, 1.0, 0.0)
            rank = _dot(e.astype(BF16), tri) + offs
            sel = jnp.logical_or(keys > thr, jnp.logical_and(eq, rank < need))
            sel_ref[p] = jnp.where(sel, 1.0, 0.0)
            return offs + jnp.sum(e, axis=1, keepdims=True)

        lax.fori_loop(0, n_pages + 1, body, jnp.zeros((n_new, 1), F32))


def _select_sample(qs, wi, pool_idx, ki_new, page_table, topk):
    s, rows, _ = qs.shape
    n_new = ki_new.shape[1]
    n_pages = page_table.shape[1]
    kernel = functools.partial(_select_sample_kernel, topk=topk, n_new=n_new)
    grid_spec = pltpu.PrefetchScalarGridSpec(
        num_scalar_prefetch=1,
        grid=(s, n_pages + 1),
        in_specs=[
            pl.BlockSpec((None, rows, IDX_DIM), lambda b, p, pt: (b, 0, 0)),
            pl.BlockSpec((None, n_new, IDX_HEADS), lambda b, p, pt: (b, 0, 0)),
            pl.BlockSpec((None, PAGE_SIZE, IDX_DIM),
                         lambda b, p, pt: (pt[b * n_pages + jnp.minimum(p, n_pages - 1)], 0, 0)),
            pl.BlockSpec((None, n_new, IDX_DIM), lambda b, p, pt: (b, 0, 0)),
        ],
        out_specs=pl.BlockSpec((None, n_pages + 1, n_new, PAGE_SIZE), lambda b, p, pt: (b, 0, 0, 0)),
        scratch_shapes=[pltpu.VMEM((n_pages + 1, n_new, PAGE_SIZE), jnp.int32)],
    )
    return pl.pallas_call(
        kernel,
        out_shape=jax.ShapeDtypeStruct((s, n_pages + 1, n_new, PAGE_SIZE), F32),
        grid_spec=grid_spec,
        compiler_params=_cparams(("arbitrary", "arbitrary")),
        name="indexer_select_sample",
    )(page_table.reshape(-1), qs, wi, pool_idx, ki_new)


def _attn_c_sample_kernel(pt_ref, wq_ref, page_ref, kvn_ref, sel_ref, o_ref, m_sc, l_sc, acc_sc,
                          *, n_new):
    kw = wq_ref.shape[1]
    n_rep = wq_ref.shape[0] // n_new
    p_id = pl.program_id(1)
    n_pages = pl.num_programs(1) - 1

    @pl.when(p_id == 0)
    def _():
        m_sc[...] = jnp.full(m_sc.shape, NEG_INF, F32)
        l_sc[...] = jnp.zeros(l_sc.shape, F32)
        acc_sc[...] = jnp.zeros(acc_sc.shape, F32)

    def update(k, v):
        sel = jnp.concatenate([sel_ref[...]] * n_rep, axis=0) > 0.5
        s = jnp.where(sel, _dot_nt(wq_ref[...], k), NEG_INF)
        m_old = m_sc[...]
        m_new = jnp.maximum(m_old, jnp.max(s, axis=1, keepdims=True))
        m_use = jnp.where(m_new == NEG_INF, 0.0, m_new)
        alpha = jnp.exp(m_old - m_use)
        p = jnp.exp(s - m_use)
        l_sc[...] = alpha * l_sc[...] + jnp.sum(p, axis=1, keepdims=True)
        acc_sc[...] = alpha * acc_sc[...] + _dot(p.astype(BF16), v)
        m_sc[...] = m_new

    @pl.when(p_id < n_pages)
    def _():
        update(page_ref[:, 0:kw].astype(BF16), page_ref[:, kw:2 * kw].astype(BF16))

    @pl.when(p_id == n_pages)
    def _():
        pad = jnp.zeros((PAGE_SIZE - n_new, kw), F32)
        update(jnp.concatenate([kvn_ref[:, 0:kw], pad], axis=0).astype(BF16),
               jnp.concatenate([kvn_ref[:, kw:2 * kw], pad], axis=0).astype(BF16))
        o_ref[...] = acc_sc[...] / l_sc[...]


def _attn_c_sample(wq, pool, kvn, sel, page_table):
    s, rows, kw = wq.shape
    n_new = kvn.shape[1]
    n_pages = page_table.shape[1]
    grid_spec = pltpu.PrefetchScalarGridSpec(
        num_scalar_prefetch=1,
        grid=(s, n_pages + 1),
        in_specs=[
            pl.BlockSpec((None, rows, kw), lambda b, p, pt: (b, 0, 0)),
            pl.BlockSpec((None, PAGE_SIZE, 2 * kw),
                         lambda b, p, pt: (pt[b * n_pages + jnp.minimum(p, n_pages - 1)], 0, 0)),
            pl.BlockSpec((None, n_new, 2 * kw), lambda b, p, pt: (b, 0, 0)),
            pl.BlockSpec((None, None, n_new, PAGE_SIZE), lambda b, p, pt: (b, p, 0, 0)),
        ],
        out_specs=pl.BlockSpec((None, rows, kw), lambda b, p, pt: (b, 0, 0)),
        scratch_shapes=[pltpu.VMEM((rows, 1), F32), pltpu.VMEM((rows, 1), F32),
                        pltpu.VMEM((rows, kw), F32)],
    )
    return pl.pallas_call(
        functools.partial(_attn_c_sample_kernel, n_new=n_new),
        out_shape=jax.ShapeDtypeStruct((s, rows, kw), F32),
        grid_spec=grid_spec,
        compiler_params=_cparams(("arbitrary", "arbitrary")),
        name="sparse_attn_sample",
    )(page_table.reshape(-1), wq, pool, kvn, sel)


C_QW = C_HEADS * HEAD_DIM
C_KW = C_KV_HEADS * HEAD_DIM
C_IW = IDX_HEADS * IDX_DIM


def _prep_c_weights(w):
    q, k, v, qi, ki, wi = jnp.split(w, [C_QW, C_QW + C_KW, C_QW + 2 * C_KW,
                                        C_QW + 2 * C_KW + C_IW, C_QW + 2 * C_KW + C_IW + IDX_DIM],
                                    axis=1)
    dup = lambda a: jnp.repeat(a.reshape(a.shape[0], C_KV_HEADS, 1, HEAD_DIM), 2, axis=2).reshape(
        a.shape[0], 2 * C_KW)
    pad = jnp.zeros((w.shape[0], LANES - IDX_DIM - IDX_HEADS), w.dtype)
    return jnp.concatenate([q, k, v, qi, ki, wi, pad, dup(k), dup(v)], axis=1)


def _plan_c():
    o_kv = C_QW
    o_qi = C_QW + 2 * C_KW
    o_ki = o_qi + C_IW
    o_kd = o_ki + LANES
    o_vd = o_kd + 2 * C_KW
    return [
        (0, 512, 512, QK_SCALE, ((0, 0),)),
        (512, 512, 512, QK_SCALE, ((0, 512),)),
        (o_kv, 2 * C_KW, C_KW, 1.0, ((1, 0),)),
        (o_qi, C_IW, C_IW, 1.0, ((2, 0),)),
        (o_ki, LANES, IDX_DIM, 1.0, ((3, 0),)),
        (o_kd, 2 * C_KW, 2 * C_KW, 1.0, ((4, 0),)),
        (o_vd, 2 * C_KW, 0, 1.0, ((5, 0),)),
    ]


def _mixer_c_prompt(x, mod3, tables, w_prep, bsz, seq_len, tm):
    outs = [(C_QW, BF16, None), (2 * C_KW, F32, None), (C_IW, BF16, None), (LANES, F32, None),
            (2 * C_KW, BF16, None), (2 * C_KW, BF16, None)]
    q, kvf, qi, kiw, kd, vd = _inproj(x, mod3, tables, w_prep, _plan_c(), outs, tm, seq_len)
    topk = min(C_TOPK_MAX, seq_len // 4)
    r3 = lambda a: a.reshape(bsz, seq_len, a.shape[-1])
    sel = _select_prompt(r3(qi), r3(kiw), topk)
    o = _attn_c_prompt(r3(q), r3(kd), r3(vd), sel, min(512, seq_len))
    n_pg = bsz * seq_len // PAGE_SIZE
    kv = kvf.reshape(n_pg, PAGE_SIZE, 2, C_KV_HEADS, HEAD_DIM)
    ki = kiw[:, :IDX_DIM].reshape(n_pg, PAGE_SIZE, IDX_DIM)
    return o.reshape(bsz * seq_len, C_QW), kv, ki


def _mixer_c_sample(x, mod3, tables, w_prep, pool_kv, pool_idx, page_table, n_seq, n_new):
    outs = [(C_QW, BF16, None), (2 * C_KW, F32, None), (C_IW, BF16, None), (LANES, F32, None),
            (2 * C_KW, BF16, None), (2 * C_KW, BF16, None)]
    q, kvf, qi, kiw, _, _ = _inproj(x, mod3, tables, w_prep, _plan_c(), outs, x.shape[0], n_new)
    n_past = page_table.shape[1] * PAGE_SIZE
    topk = min(C_TOPK_MAX, (n_past + n_new) // 4)
    qs = qi.reshape(n_seq, n_new, IDX_HEADS, IDX_DIM).transpose(0, 2, 1, 3).reshape(
        n_seq, IDX_HEADS * n_new, IDX_DIM)
    kiw3 = kiw.reshape(n_seq, n_new, LANES)
    ki_new = kiw3[:, :, :IDX_DIM]
    wi = kiw3[:, :, IDX_DIM:IDX_DIM + IDX_HEADS]
    sel = _select_sample(qs, wi, pool_idx, ki_new, page_table, topk)
    q4 = q.reshape(n_seq, n_new, C_HEADS, HEAD_DIM).transpose(0, 2, 1, 3)
    kv_of = jnp.arange(C_HEADS) // (C_HEADS // C_KV_HEADS)
    onehot = (kv_of[:, None] == jnp.arange(C_KV_HEADS)[None, :]).astype(BF16)
    wq = (q4[:, :, :, None, :] * onehot[None, :, None, :, None]).reshape(
        n_seq, C_HEADS * n_new, C_KW)
    kvn = kvf.reshape(n_seq, n_new, 2 * C_KW)
    acc = _attn_c_sample(wq, pool_kv.reshape(pool_kv.shape[0], PAGE_SIZE, 2 * C_KW), kvn, sel,
                         page_table)
    acc = acc.reshape(n_seq, C_HEADS, n_new, C_KV_HEADS, HEAD_DIM)
    o = jnp.einsum("shjad,ha->sjhd", acc, onehot.astype(F32))
    o = o.reshape(n_seq * n_new, C_QW).astype(BF16)
    return (o, kvf.reshape(n_seq, n_new, 2, C_KV_HEADS, HEAD_DIM),
            ki_new.reshape(n_seq, n_new, IDX_DIM))


def kernel(x_prompt, x_sample, state_a_kv_w128, state_a_kv_w512, state_a_kv_w2048, cache_b_kv,
           cache_c_kv, cache_c_idx, page_table, c_prompt, c_sample, w_mod, b_mod, ln_g, ln_b,
           w_ffn_in, w_ffn_out, a_w_in, a_w_out, b_w_in, b_w_out, b_lambda, b_subln_g, c_w_in,
           c_w_out):
    bsz, seq_len, d = x_prompt.shape
    n_seq, n_new, _ = x_sample.shape
    n_past = page_table.shape[1] * PAGE_SIZE
    a_states = (state_a_kv_w128, state_a_kv_w512, state_a_kv_w2048)
    tm = min(512, seq_len)
    rs = n_seq * n_new

    n_c = bsz + n_seq
    n_c_pad = -(-n_c // 8) * 8
    c_all = jnp.concatenate([c_prompt, c_sample, jnp.zeros((n_c_pad - n_c, d), F32)], axis=0)
    mod_all = _modulation(c_all, w_mod, b_mod)

    tables_p = _rope_tables(jnp.arange(seq_len, dtype=jnp.int32))
    tables_s = tuple(jnp.tile(t, (n_seq, 1)) for t in
                     _rope_tables(n_past + jnp.arange(n_new, dtype=jnp.int32)))

    xp = x_prompt.reshape(bsz * seq_len, d)
    xs = x_sample.reshape(rs, d)
    new_a_p = [[] for _ in A_GROUPS]
    new_a_s = [[] for _ in A_GROUPS]
    b_p, b_s, ckv_p, ckv_s, cidx_p, cidx_s = [], [], [], [], [], []

    for i in range(DEPTH):
        kind, j = i % N_MIXERS, i // N_MIXERS
        mod_p = mod_all[i, :bsz][:, None, :]
        mod_s = jnp.repeat(mod_all[i, bsz:bsz + n_seq], n_new, axis=0)[None]
        if kind == 0:
            w_in = a_w_in[j].astype(BF16)
            w_out = a_w_out[j].astype(BF16)
            op, bufs_p = _mixer_a_prompt(xp, mod_p, tables_p, w_in, bsz, seq_len, tm)
            os_, bufs_s = _mixer_a_sample(xs, mod_s, tables_s, w_in, [s[j] for s in a_states],
                                          n_seq, n_new)
            for g in range(len(A_GROUPS)):
                new_a_p[g].append(bufs_p[g])
                new_a_s[g].append(bufs_s[g])
        elif kind == 1:
            w_in = b_w_in[j].astype(BF16)
            w_out = b_w_out[j].astype(BF16)
            lam_init = 0.8 - 0.6 * math.exp(-0.3 * i)
            op, kv_p = _mixer_b_prompt(xp, mod_p, tables_p, w_in, b_lambda[j], b_subln_g[j],
                                       lam_init, bsz, seq_len, tm)
            os_, kv_s = _mixer_b_sample(xs, mod_s, tables_s, w_in, b_lambda[j], b_subln_g[j],
                                        lam_init, cache_b_kv[j], page_table, n_seq, n_new)
            b_p.append(kv_p)
            b_s.append(kv_s)
        else:
            w_in = _prep_c_weights(c_w_in[j]).astype(BF16)
            w_out = c_w_out[j].astype(BF16)
            op, kv_p, ki_p = _mixer_c_prompt(xp, mod_p, tables_p, w_in, bsz, seq_len, tm)
            os_, kv_s, ki_s = _mixer_c_sample(xs, mod_s, tables_s, w_in, cache_c_kv[j],
                                              cache_c_idx[j], page_table, n_seq, n_new)
            ckv_p.append(kv_p)
            ckv_s.append(kv_s)
            cidx_p.append(ki_p)
            cidx_s.append(ki_s)
        w1 = w_ffn_in[i].astype(BF16)
        w2 = w_ffn_out[i].astype(BF16)
        xp = _post(xp, op, mod_p, w_out, ln_g[i], ln_b[i], w1, w2, tm, seq_len)
        xs = _post(xs, os_, mod_s, w_out, ln_g[i], ln_b[i], w1, w2, rs, n_new)

    y_prompt = xp.reshape(bsz, seq_len, d)
    y_sample = xs.reshape(n_seq, n_new, d)
    a_p = [jnp.stack(a) for a in new_a_p]
    a_s = [jnp.stack(a) for a in new_a_s]
    return (y_prompt, y_sample, a_p[0], a_s[0], a_p[1], a_s[1], a_p[2], a_s[2],
            jnp.stack(b_p), jnp.stack(b_s), jnp.stack(ckv_p), jnp.stack(ckv_s),
            jnp.stack(cidx_p), jnp.stack(cidx_s))
```

```python
import functools
import math

import jax
import jax.numpy as jnp
from jax import lax
from jax.experimental import pallas as pl
from jax.experimental.pallas import tpu as pltpu

F32 = jnp.float32
BF16 = jnp.bfloat16

DEPTH = 4
N_MIXERS = 3
D_MODEL = 1024
HEAD_DIM = 64
PAGE_SIZE = 128
ROPE_THETA = 500000.0
LN_EPS = 1e-5
DEEPNORM_ALPHA = (2 * DEPTH) ** 0.25
QK_SCALE = HEAD_DIM ** -0.5

A_GROUPS = ((128, 1), (512, 4), (2048, 16))
A_HEADS = 8
A_WIDTH = A_HEADS * HEAD_DIM
B_HEADS = 8
B_VDIM = 2 * HEAD_DIM
C_HEADS = 16
C_KV_HEADS = 4
IDX_HEADS = 8
IDX_DIM = 64
C_TOPK_MAX = 256
IDX_W_SCALE = IDX_HEADS ** -0.5 * IDX_DIM ** -0.5
FFN_HIDDEN = 2816

LANES = 128
VMEM_LIMIT_BYTES = 56 * 1024 * 1024
NEG_INF = float("-inf")
INT_MIN = -(2 ** 31)


def _cparams(semantics, vmem=VMEM_LIMIT_BYTES):
    return pltpu.CompilerParams(dimension_semantics=semantics, vmem_limit_bytes=vmem)


def _dot(a, b):
    return jnp.dot(a, b, preferred_element_type=F32)


def _dot_nt(a, b):
    return lax.dot_general(a, b, (((1,), (1,)), ((), ())), preferred_element_type=F32)


def _dot_tn(a, b):
    return lax.dot_general(a, b, (((0,), (0,)), ((), ())), preferred_element_type=F32)


def _resident(shape):
    nd = len(shape)
    return pl.BlockSpec(shape, lambda *_: (0,) * nd, pipeline_mode=pl.Buffered(1))


def _split_bf16(a):
    hi = a.astype(BF16)
    lo = (a - hi.astype(F32)).astype(BF16)
    return hi, lo


def _mod_kernel(c_ref, w_ref, b_ref, o_ref):
    c = c_ref[...]
    a_hi, a_lo = _split_bf16(jax.nn.silu(c))
    w_hi, w_lo = _split_bf16(w_ref[...])
    o_ref[...] = _dot(a_hi, w_hi) + _dot(a_hi, w_lo) + _dot(a_lo, w_hi) + b_ref[...]


def _modulation(c_all, w_mod, b_mod):
    n, d = c_all.shape
    depth, _, n6 = w_mod.shape
    tn = 1536
    return pl.pallas_call(
        _mod_kernel,
        out_shape=jax.ShapeDtypeStruct((depth, n, n6), F32),
        grid=(depth, n6 // tn),
        in_specs=[
            pl.BlockSpec((n, d), lambda i, j: (0, 0)),
            pl.BlockSpec((None, d, tn), lambda i, j: (i, 0, j)),
            pl.BlockSpec((None, 1, tn), lambda i, j: (i, 0, j)),
        ],
        out_specs=pl.BlockSpec((None, n, tn), lambda i, j: (i, 0, j)),
        compiler_params=_cparams(("arbitrary", "arbitrary")),
        name="adaln_modulation",
    )(c_all, w_mod, b_mod.reshape(depth, 1, n6))


def _rope_tables(pos):
    rot = HEAD_DIM // 4
    half = rot // 2
    inv = ROPE_THETA ** (-jnp.arange(0, rot, 2, dtype=F32) / rot)
    ang = pos.astype(F32)[:, None] * inv[None, :]
    cos, sin = jnp.cos(ang), jnp.sin(ang)
    n = pos.shape[0]
    ones = jnp.ones((n, HEAD_DIM - rot), F32)
    zeros = jnp.zeros((n, HEAD_DIM - rot), F32)
    zh = jnp.zeros((n, half), F32)
    c = jnp.concatenate([cos, cos, ones], axis=1)
    sa = jnp.concatenate([-sin, zh, zeros], axis=1)
    sb = jnp.concatenate([zh, sin, zeros], axis=1)
    rep = LANES // HEAD_DIM
    return jnp.tile(c, (1, rep)), jnp.tile(sa, (1, rep)), jnp.tile(sb, (1, rep))


INPROJ_SEG = 512


def _rows_to_lanes(y, o_ref, col, dil, stage_ref):
    tm, wd = y.shape
    slot = o_ref.shape[1] // dil
    for c in range(wd // LANES):
        stage_ref[c] = y[:, c * LANES:(c + 1) * LANES]
    for r in range(dil):
        for c in range(wd // LANES):
            lo = r * slot + col + c * LANES
            o_ref[:, lo:lo + LANES] = stage_ref[c, pl.ds(r, tm // dil, stride=dil), :].astype(
                o_ref.dtype)


def _lanes_to_rows(x_ref, dil, stage_ref):
    n, tot = x_ref.shape
    wd = tot // dil
    for r in range(dil):
        for c in range(wd // LANES):
            lo = r * wd + c * LANES
            stage_ref[c, pl.ds(r, n, stride=dil), :] = x_ref[:, lo:lo + LANES].astype(F32)
    return jnp.concatenate([stage_ref[c] for c in range(wd // LANES)], axis=1)


def _inproj_kernel(x_ref, mod_ref, cos_ref, sa_ref, sb_ref, w_ref, *refs,
                   plan, keeps, tiles_per_seq):
    out_refs, stage_ref = refs[:-1], refs[-1]
    d = x_ref.shape[1]
    tm = x_ref.shape[0]
    x = x_ref[...]
    h = (x * (1.0 + mod_ref[:, d:2 * d]) + mod_ref[:, 0:d]).astype(BF16)
    cos, sa, sb = cos_ref[...], sa_ref[...], sb_ref[...]
    half = HEAD_DIM // 8
    t_in = pl.program_id(0) % tiles_per_seq
    for c0, wd, rope_cols, scale, dests in plan:
        y = _dot(h, w_ref[:, c0:c0 + wd])
        if rope_cols:
            reps = wd // LANES
            tile = lambda t: jnp.concatenate([t] * reps, axis=1) if reps > 1 else t
            yr = (y * tile(cos) + pltpu.roll(y, wd - half, 1) * tile(sa)
                  + pltpu.roll(y, half, 1) * tile(sb))
            if rope_cols < wd:
                lane = lax.broadcasted_iota(jnp.int32, y.shape, 1)
                y = jnp.where(lane < rope_cols, yr, y)
            else:
                y = yr
        if scale != 1.0:
            y = y * scale
        for oi, col in dests:
            o_ref = out_refs[oi]
            keep = keeps[oi]
            if keep is None:
                o_ref[:, col:col + wd] = y.astype(o_ref.dtype)
            elif isinstance(keep, tuple):
                if keep[1] == 1:
                    o_ref[:, col:col + wd] = y.astype(o_ref.dtype)
                else:
                    _rows_to_lanes(y, o_ref, col, keep[1], stage_ref)
            elif keep >= tm:
                first = tiles_per_seq - keep // tm

                @pl.when(t_in >= first)
                def _(o_ref=o_ref, y=y, col=col, wd=wd):
                    o_ref[:, col:col + wd] = y.astype(o_ref.dtype)
            else:
                @pl.when(t_in == tiles_per_seq - 1)
                def _(o_ref=o_ref, y=y, col=col, wd=wd, keep=keep):
                    o_ref[:, col:col + wd] = y[tm - keep:, :].astype(o_ref.dtype)


def _inproj(x, mod3, tables, w, plan, outs, tm, seq_len):
    r, d = x.shape
    n_tiles = r // tm
    tps = seq_len // tm if mod3.shape[1] == 1 else 1
    tbl_tiles = tables[0].shape[0] // tm
    rm = mod3.shape[1]
    out_shapes, out_specs, keeps = [], [], []
    for cols, dtype, keep in outs:
        keeps.append(keep)
        if keep is None:
            out_shapes.append(jax.ShapeDtypeStruct((r, cols), dtype))
            out_specs.append(pl.BlockSpec((tm, cols), lambda t: (t, 0)))
        elif isinstance(keep, tuple):
            dil = keep[1]
            assert tm % (16 * dil) == 0
            out_shapes.append(jax.ShapeDtypeStruct((r // dil, dil * cols), dtype))
            out_specs.append(pl.BlockSpec((tm // dil, dil * cols), lambda t: (t, 0)))
        elif keep >= tm:
            assert keep % tm == 0 and mod3.shape[1] == 1
            first = tps - keep // tm
            out_shapes.append(jax.ShapeDtypeStruct((r // seq_len, keep, cols), dtype))
            out_specs.append(pl.BlockSpec(
                (None, tm, cols),
                lambda t, first=first: (t // tps, jnp.maximum(t % tps - first, 0), 0)))
        else:
            assert tm % keep == 0 and keep % 8 == 0 and mod3.shape[1] == 1
            out_shapes.append(jax.ShapeDtypeStruct((r // seq_len, keep, cols), dtype))
            out_specs.append(pl.BlockSpec((None, keep, cols), lambda t: (t // tps, 0, 0)))
    tbl_spec = pl.BlockSpec((tm, LANES), lambda t: (t % tbl_tiles, 0))
    kernel = functools.partial(_inproj_kernel, plan=tuple(plan), keeps=tuple(keeps),
                               tiles_per_seq=tps)
    return pl.pallas_call(
        kernel,
        out_shape=out_shapes,
        grid=(n_tiles,),
        in_specs=[
            pl.BlockSpec((tm, d), lambda t: (t, 0)),
            pl.BlockSpec((None, rm, 2 * d), lambda t: (t // tps, 0, 0)),
            tbl_spec, tbl_spec, tbl_spec,
            _resident(w.shape),
        ],
        out_specs=out_specs,
        scratch_shapes=[pltpu.VMEM((INPROJ_SEG // LANES, tm, LANES), F32)],
        compiler_params=_cparams(("arbitrary",)),
        name="mixer_in_proj",
    )(x, mod3, *tables, w)


def _layer_norm(x, g, b):
    mu = jnp.mean(x, axis=-1, keepdims=True)
    xc = x - mu
    var = jnp.mean(xc * xc, axis=-1, keepdims=True)
    return xc * lax.rsqrt(var + LN_EPS) * g + b


FFN_CHUNK = 256


def _post_kernel(x_ref, o_ref, mod_ref, wo_ref, lng_ref, lnb_ref, w1_ref, w2_ref,
                 y_ref, acc_ref):
    d = x_ref.shape[1]
    f = w2_ref.shape[0]
    x = x_ref[...]
    g1 = mod_ref[:, 2 * d:3 * d]
    sh2 = mod_ref[:, 3 * d:4 * d]
    sc2 = mod_ref[:, 4 * d:5 * d]
    g2 = mod_ref[:, 5 * d:6 * d]
    m = _dot(o_ref[...], wo_ref[...])
    x1 = _layer_norm(DEEPNORM_ALPHA * x + (1.0 + g1) * m, lng_ref[0:1, :], lnb_ref[0:1, :])
    h = (x1 * (1.0 + sc2) + sh2).astype(BF16)
    for c in range(f // FFN_CHUNK):
        c0 = c * FFN_CHUNK
        g = _dot(h, w1_ref[:, c0:c0 + FFN_CHUNK])
        u = _dot(h, w1_ref[:, f + c0:f + c0 + FFN_CHUNK])
        part = _dot((jax.nn.silu(g) * u).astype(BF16), w2_ref[c0:c0 + FFN_CHUNK, :])
        if c == 0:
            acc_ref[...] = part
        else:
            acc_ref[...] += part
    y_ref[...] = _layer_norm(DEEPNORM_ALPHA * x1 + (1.0 + g2) * acc_ref[...],
                             lng_ref[1:2, :], lnb_ref[1:2, :])


def _post(x, o, mod3, wo, lng, lnb, w1, w2, tm, seq_len):
    r, d = x.shape
    wo_in = o.shape[1]
    rm = mod3.shape[1]
    tps = seq_len // tm if rm == 1 else 1
    return pl.pallas_call(
        _post_kernel,
        out_shape=jax.ShapeDtypeStruct((r, d), F32),
        grid=(r // tm,),
        in_specs=[
            pl.BlockSpec((tm, d), lambda t: (t, 0)),
            pl.BlockSpec((tm, wo_in), lambda t: (t, 0)),
            pl.BlockSpec((None, rm, 6 * d), lambda t: (t // tps, 0, 0)),
            _resident(wo.shape), _resident(lng.shape), _resident(lnb.shape),
            _resident(w1.shape), _resident(w2.shape),
        ],
        out_specs=pl.BlockSpec((tm, d), lambda t: (t, 0)),
        scratch_shapes=[pltpu.VMEM((tm, d), F32)],
        compiler_params=_cparams(("arbitrary",)),
        name="out_proj_ffn",
    )(x, o, mod3, wo, lng, lnb, w1, w2)


def _pair_stack(qp):
    lane = lax.broadcasted_iota(jnp.int32, qp.shape, 1)
    lo = lane < HEAD_DIM
    zero = jnp.zeros_like(qp)
    return jnp.concatenate([jnp.where(lo, qp, zero), jnp.where(lo, zero, qp)], axis=0)


def _pair_merge(top, bot):
    lane = lax.broadcasted_iota(jnp.int32, top.shape, 1)
    return jnp.where(lane < HEAD_DIM, top, bot)


def _attn_a_prompt_kernel(q_ref, kvc_ref, kvp_ref, o_ref, lse_ref):
    tq = q_ref.shape[0]
    width = q_ref.shape[1]
    i = pl.program_id(2)
    krow = lax.broadcasted_iota(jnp.int32, (tq, 2 * tq), 0)
    qcol = lax.broadcasted_iota(jnp.int32, (tq, 2 * tq), 1) % tq
    mask_c = krow <= qcol
    mask_p = jnp.logical_and(krow >= qcol, i > 0)
    for j in range(width // LANES):
        sl = slice(j * LANES, (j + 1) * LANES)
        vsl = slice(width + j * LANES, width + (j + 1) * LANES)
        qs = _pair_stack(q_ref[:, sl])
        s_c = jnp.where(mask_c, _dot_nt(kvc_ref[:, sl], qs), NEG_INF)
        s_p = jnp.where(mask_p, _dot_nt(kvp_ref[:, sl], qs), NEG_INF)
        m = jnp.maximum(jnp.max(s_c, axis=0, keepdims=True), jnp.max(s_p, axis=0, keepdims=True))
        p_c = jnp.exp(s_c - m)
        p_p = jnp.exp(s_p - m)
        l = jnp.sum(p_c, axis=0, keepdims=True) + jnp.sum(p_p, axis=0, keepdims=True)
        ov = (_dot_tn(kvc_ref[:, vsl], p_c.astype(BF16))
              + _dot_tn(kvp_ref[:, vsl], p_p.astype(BF16))) / l
        lse = jnp.broadcast_to(m + jnp.log(l), (HEAD_DIM, 2 * tq))
        o_ref[:, sl] = jnp.transpose(
            jnp.concatenate([ov[:HEAD_DIM, :tq], ov[HEAD_DIM:, tq:]], axis=0))
        lse_ref[:, sl] = jnp.transpose(jnp.concatenate([lse[:, :tq], lse[:, tq:]], axis=0))


def _attn_a_prompt(q, kv, dil):
    b, ld, tot = q.shape
    width = tot // dil
    tq = PAGE_SIZE
    nb = ld // tq
    o, lse = pl.pallas_call(
        _attn_a_prompt_kernel,
        out_shape=[jax.ShapeDtypeStruct(q.shape, F32), jax.ShapeDtypeStruct(q.shape, F32)],
        grid=(b, dil, nb),
        in_specs=[
            pl.BlockSpec((None, tq, width), lambda b_, r, i: (b_, i, r)),
            pl.BlockSpec((None, tq, 2 * width), lambda b_, r, i: (b_, i, r)),
            pl.BlockSpec((None, tq, 2 * width), lambda b_, r, i: (b_, jnp.maximum(i - 1, 0), r)),
        ],
        out_specs=[pl.BlockSpec((None, tq, width), lambda b_, r, i: (b_, i, r))] * 2,
        compiler_params=_cparams(("arbitrary",) * 3),
        name="dilated_attn_prompt",
    )(q, kv, kv)
    return o, lse


def _attn_a_sample_kernel(q_ref, st_ref, kvn_ref, o_ref, lse_ref, *, dil):
    nq, width = q_ref.shape
    w = st_ref.shape[0]
    pad = LANES - nq
    jq = lax.broadcasted_iota(jnp.int32, (2 * nq, w), 0) % nq
    ik = lax.broadcasted_iota(jnp.int32, (2 * nq, w), 1)
    off = w + jq - ik
    mask_b = jnp.logical_and(off <= w, (off & (dil - 1)) == 0)
    jq_n = lax.broadcasted_iota(jnp.int32, (2 * nq, LANES), 0) % nq
    jn = lax.broadcasted_iota(jnp.int32, (2 * nq, LANES), 1)
    off_n = jq_n - jn
    mask_n = jnp.logical_and(jnp.logical_and(off_n >= 0, jn < nq), (off_n & (dil - 1)) == 0)
    for j in range(width // LANES):
        sl = slice(j * LANES, (j + 1) * LANES)
        vsl = slice(width + j * LANES, width + (j + 1) * LANES)
        qs = _pair_stack(q_ref[:, sl]).astype(BF16)
        kb = st_ref[:, sl].astype(BF16)
        vb = st_ref[:, vsl].astype(BF16)
        zpad = jnp.zeros((pad, LANES), F32)
        kn = jnp.concatenate([kvn_ref[:, sl], zpad], axis=0).astype(BF16)
        vn = jnp.concatenate([kvn_ref[:, vsl], zpad], axis=0).astype(BF16)
        s_b = jnp.where(mask_b, _dot_nt(qs, kb), NEG_INF)
        s_n = jnp.where(mask_n, _dot_nt(qs, kn), NEG_INF)
        m = jnp.maximum(jnp.max(s_b, axis=1, keepdims=True), jnp.max(s_n, axis=1, keepdims=True))
        p_b = jnp.exp(s_b - m)
        p_n = jnp.exp(s_n - m)
        l = jnp.sum(p_b, axis=1, keepdims=True) + jnp.sum(p_n, axis=1, keepdims=True)
        ov = (_dot(p_b.astype(BF16), vb) + _dot(p_n.astype(BF16), vn)) / l
        lse = jnp.broadcast_to(m + jnp.log(l), ov.shape)
        o_ref[:, sl] = _pair_merge(ov[:nq], ov[nq:])
        lse_ref[:, sl] = _pair_merge(lse[:nq], lse[nq:])


def _attn_a_sample(q, state, seq_base, kvn, dil):
    s, nq, width = q.shape
    w = state.shape[1]
    return pl.pallas_call(
        functools.partial(_attn_a_sample_kernel, dil=dil),
        out_shape=[jax.ShapeDtypeStruct(q.shape, F32)] * 2,
        grid=(s,),
        in_specs=[
            pl.BlockSpec((None, nq, width), lambda b: (b, 0, 0)),
            pl.BlockSpec((None, w, 2 * width), lambda b: (seq_base + b, 0, 0)),
            pl.BlockSpec((None, nq, 2 * width), lambda b: (b, 0, 0)),
        ],
        out_specs=[pl.BlockSpec((None, nq, width), lambda b: (b, 0, 0))] * 2,
        compiler_params=_cparams(("arbitrary",)),
        name="dilated_attn_sample",
    )(q, state, kvn)


def _shift_state_kernel(s_ref, o_ref, *, n_new):
    w = s_ref.shape[0]
    o_ref[0:w - n_new, :] = s_ref[n_new:w, :]
    o_ref[w - n_new:w, :] = jnp.zeros((n_new, s_ref.shape[1]), o_ref.dtype)


def _shift_state(state, n_new):
    n, w, c = state.shape
    spec = pl.BlockSpec((None, w, c), lambda b: (b, 0, 0))
    return pl.pallas_call(
        functools.partial(_shift_state_kernel, n_new=n_new),
        out_shape=jax.ShapeDtypeStruct(state.shape, state.dtype),
        grid=(n,),
        in_specs=[spec],
        out_specs=spec,
        compiler_params=_cparams(("arbitrary",)),
        name="window_shift",
    )(state)


def _set_tail_kernel(new_ref, buf_ref, o_ref):
    del buf_ref
    o_ref[...] = new_ref[...]


def _set_tail(buf, new_rows, seq_base):
    s, n_new, c = new_rows.shape
    w = buf.shape[1]
    assert w % n_new == 0
    return pl.pallas_call(
        _set_tail_kernel,
        out_shape=jax.ShapeDtypeStruct(buf.shape, buf.dtype),
        grid=(s,),
        in_specs=[pl.BlockSpec((None, n_new, c), lambda b: (b, 0, 0)),
                  pl.BlockSpec(memory_space=pl.ANY)],
        out_specs=pl.BlockSpec((None, n_new, c), lambda b: (seq_base + b, w // n_new - 1, 0)),
        input_output_aliases={1: 0},
        compiler_params=_cparams(("arbitrary",)),
        name="window_append",
    )(new_rows, buf)


def _combine_a_kernel(*refs, dils):
    n = len(dils)
    o_refs, l_refs, out_ref, stage_ref = refs[:n], refs[n:2 * n], refs[2 * n], refs[2 * n + 1]
    load = lambda ref, dil: ref[...] if dil == 1 else _lanes_to_rows(ref, dil, stage_ref)
    lses = [load(r, dil) for r, dil in zip(l_refs, dils)]
    m = functools.reduce(jnp.maximum, lses)
    ws = [jnp.exp(l - m) for l in lses]
    num = sum(w * load(r, dil) for w, r, dil in zip(ws, o_refs, dils))
    out_ref[...] = (num / sum(ws)).astype(out_ref.dtype)


def _combine_a(os_, lses, dils, tm):
    width = os_[0].shape[1] // dils[0]
    r = os_[0].shape[0] * dils[0]
    specs = [pl.BlockSpec((tm // dil, dil * width), lambda t: (t, 0)) for dil in dils]
    return pl.pallas_call(
        functools.partial(_combine_a_kernel, dils=tuple(dils)),
        out_shape=jax.ShapeDtypeStruct((r, width), BF16),
        grid=(r // tm,),
        in_specs=specs * 2,
        out_specs=pl.BlockSpec((tm, width), lambda t: (t, 0)),
        scratch_shapes=[pltpu.VMEM((width // LANES, tm, LANES), F32)],
        compiler_params=_cparams(("arbitrary",)),
        name="dilated_combine",
    )(*os_, *lses)


def _plan_a():
    plan = []
    for g in range(len(A_GROUPS)):
        base = g * 3 * A_WIDTH
        plan.append((base, A_WIDTH, A_WIDTH, QK_SCALE, ((g, 0),)))
        plan.append((base + A_WIDTH, A_WIDTH, A_WIDTH, 1.0, ((3 + g, 0), (6 + g, 0))))
        plan.append((base + 2 * A_WIDTH, A_WIDTH, 0, 1.0, ((3 + g, A_WIDTH), (6 + g, A_WIDTH))))
    return plan


def _mixer_a_prompt(x, mod3, tables, w_in, bsz, seq_len, tm):
    keeps = [min(win, seq_len) for win, _ in A_GROUPS]
    dils = [dil for _, dil in A_GROUPS]
    outs = ([(A_WIDTH, BF16, ("dil", dil)) for dil in dils]
            + [(2 * A_WIDTH, BF16, ("dil", dil)) for dil in dils]
            + [(2 * A_WIDTH, F32, k) for k in keeps])
    res = _inproj(x, mod3, tables, w_in, _plan_a(), outs, tm, seq_len)
    os_, lses = [], []
    for g, dil in enumerate(dils):
        ld = seq_len // dil
        o, lse = _attn_a_prompt(res[g].reshape(bsz, ld, dil * A_WIDTH),
                                res[3 + g].reshape(bsz, ld, 2 * dil * A_WIDTH), dil)
        os_.append(o.reshape(bsz * ld, dil * A_WIDTH))
        lses.append(lse.reshape(bsz * ld, dil * A_WIDTH))
    o = _combine_a(os_, lses, dils, tm)
    bufs = [res[6 + g].reshape(bsz, keeps[g], 2, A_HEADS, HEAD_DIM) for g in range(3)]
    return o, bufs


def _mixer_a_sample(x, mod3, tables, w_in, states, bufs, j, n_seq, n_new):
    outs = [(A_WIDTH, F32, None)] * 3 + [(2 * A_WIDTH, F32, None)] * 3 + [(2 * A_WIDTH, F32, None)] * 3
    res = _inproj(x, mod3, tables, w_in, _plan_a(), outs, x.shape[0], n_new)
    os_, lses, new_bufs = [], [], []
    for g, (_, dil) in enumerate(A_GROUPS):
        kvn = res[3 + g].reshape(n_seq, n_new, 2 * A_WIDTH)
        o, lse = _attn_a_sample(res[g].reshape(n_seq, n_new, A_WIDTH), states[g], j * n_seq,
                                kvn, dil)
        os_.append(o.reshape(n_seq * n_new, A_WIDTH))
        lses.append(lse.reshape(n_seq * n_new, A_WIDTH))
        new_bufs.append(_set_tail(bufs[g], kvn, j * n_seq))
    o = _combine_a(os_, lses, [1] * len(A_GROUPS), x.shape[0])
    return o, new_bufs


def _lambda_of(lam_ref, lam_init):
    lp = lam_ref[...]
    a = jnp.sum(lp[0:1, :] * lp[1:2, :], axis=1, keepdims=True)
    b = jnp.sum(lp[2:3, :] * lp[3:4, :], axis=1, keepdims=True)
    return jnp.exp(a) - jnp.exp(b) + lam_init


def _sub_ln(of, g, lam_init):
    ms = jnp.mean(of * of, axis=-1, keepdims=True)
    return of * lax.rsqrt(ms + LN_EPS) * g * (1.0 - lam_init)


def _init_softmax_state(m_sc, l_sc, acc_sc):
    m_sc[...] = jnp.full(m_sc.shape, NEG_INF, F32)
    l_sc[...] = jnp.zeros(l_sc.shape, F32)
    acc_sc[...] = jnp.zeros(acc_sc.shape, F32)


def _softmax_step_t(s, v, m_sc, l_sc, acc_sc, may_be_empty=False, cols=slice(None)):
    m_old = m_sc[:, cols]
    m_new = jnp.maximum(m_old, jnp.max(s, axis=0, keepdims=True))
    m_use = jnp.where(m_new == NEG_INF, 0.0, m_new) if may_be_empty else m_new
    alpha = jnp.exp(m_old - m_use)
    p = jnp.exp(s - m_use)
    l_sc[:, cols] = alpha * l_sc[:, cols] + jnp.sum(p, axis=0, keepdims=True)
    acc_sc[:, cols] = alpha * acc_sc[:, cols] + _dot_tn(v, p.astype(BF16))
    m_sc[:, cols] = m_new


def _attn_b_prompt_kernel(qi_tab, ki_tab, q_ref, k_ref, v_ref, lam_ref, g_ref, o_ref,
                          m_sc, l_sc, acc_sc, *, lam_init):
    tq = q_ref.shape[0]
    t = pl.program_id(2)
    qi = qi_tab[t]
    ki = ki_tab[t]

    @pl.when(ki == 0)
    def _():
        _init_softmax_state(m_sc, l_sc, acc_sc)

    def step(diagonal):
        qs = _pair_stack(q_ref[...])
        s = _dot_nt(k_ref[...], qs)
        if diagonal:
            kpos = lax.broadcasted_iota(jnp.int32, s.shape, 0)
            qpos = lax.broadcasted_iota(jnp.int32, s.shape, 1) % tq
            s = jnp.where(kpos <= qpos, s, NEG_INF)
        _softmax_step_t(s, v_ref[...], m_sc, l_sc, acc_sc)

    @pl.when(ki < qi)
    def _():
        step(False)

    @pl.when(ki == qi)
    def _():
        step(True)
        lam = _lambda_of(lam_ref, lam_init)
        o = acc_sc[...] / l_sc[...]
        of = o[:, :tq] - lam * o[:, tq:]
        ms = jnp.mean(of * of, axis=0, keepdims=True)
        of = jnp.transpose(of * lax.rsqrt(ms + LN_EPS))
        o_ref[...] = (of * g_ref[...] * (1.0 - lam_init)).astype(o_ref.dtype)


def _attn_b_prompt(q, kv, lam_p, subln_g, lam_init, tq):
    b, l, hw = q.shape
    nh = hw // B_VDIM
    nq = l // tq
    pairs = [(i, k) for i in range(nq) for k in range(i + 1)]
    qi_tab = jnp.array([p[0] for p in pairs], jnp.int32)
    ki_tab = jnp.array([p[1] for p in pairs], jnp.int32)
    kernel = functools.partial(_attn_b_prompt_kernel, lam_init=lam_init)
    grid_spec = pltpu.PrefetchScalarGridSpec(
        num_scalar_prefetch=2,
        grid=(b, nh, len(pairs)),
        in_specs=[
            pl.BlockSpec((None, tq, B_VDIM), lambda b_, h, t, qt, kt: (b_, qt[t], h)),
            pl.BlockSpec((None, tq, B_VDIM), lambda b_, h, t, qt, kt: (b_, kt[t], h)),
            pl.BlockSpec((None, tq, B_VDIM), lambda b_, h, t, qt, kt: (b_, kt[t], nh + h)),
            pl.BlockSpec((4, HEAD_DIM), lambda *_: (0, 0)),
            pl.BlockSpec((1, B_VDIM), lambda *_: (0, 0)),
        ],
        out_specs=pl.BlockSpec((None, tq, B_VDIM), lambda b_, h, t, qt, kt: (b_, qt[t], h)),
        scratch_shapes=[pltpu.VMEM((1, 2 * tq), F32), pltpu.VMEM((1, 2 * tq), F32),
                        pltpu.VMEM((B_VDIM, 2 * tq), F32)],
    )
    return pl.pallas_call(
        kernel,
        out_shape=jax.ShapeDtypeStruct(q.shape, BF16),
        grid_spec=grid_spec,
        compiler_params=_cparams(("arbitrary",) * 3),
        name="diff_attn_prompt",
    )(qi_tab, ki_tab, q, kv, kv, lam_p, subln_g.reshape(1, B_VDIM))


def _page_specs(n_per_step, n_pages, page_base, block):
    n_steps = n_pages // n_per_step

    def spec(e):
        def index(b, p, pt):
            page = jnp.minimum(p, n_steps - 1) * n_per_step + e
            return (page_base + pt[b * n_pages + page],) + (0,) * (len(block) - 1)
        return pl.BlockSpec(block, index)

    return [spec(e) for e in range(n_per_step)]


def _pad_rows(a, rows):
    return jnp.concatenate([a, jnp.zeros((rows - a.shape[0], a.shape[1]), a.dtype)], axis=0)


B_PAGES_PER_STEP = 4


def _attn_b_sample_kernel(pt_ref, wq_ref, *refs, lam_init, n_new, n_per_step):
    page_refs = refs[:n_per_step]
    kvn_ref, lam_ref, g_ref, o_ref, m_sc, l_sc, acc_sc = refs[n_per_step:]
    hw = wq_ref.shape[1]
    p_id = pl.program_id(1)
    n_steps = pl.num_programs(1) - 1

    @pl.when(p_id == 0)
    def _():
        _init_softmax_state(m_sc, l_sc, acc_sc)

    @pl.when(p_id < n_steps)
    def _():
        k = jnp.concatenate([r[:, 0:hw].astype(BF16) for r in page_refs], axis=0)
        v = jnp.concatenate([r[:, hw:2 * hw].astype(BF16) for r in page_refs], axis=0)
        _softmax_step_t(_dot_nt(k, wq_ref[...]), v, m_sc, l_sc, acc_sc)

    @pl.when(p_id == n_steps)
    def _():
        k = _pad_rows(kvn_ref[:, 0:hw], PAGE_SIZE).astype(BF16)
        v = _pad_rows(kvn_ref[:, hw:2 * hw], PAGE_SIZE).astype(BF16)
        s = _dot_nt(k, wq_ref[...])
        jk = lax.broadcasted_iota(jnp.int32, s.shape, 0)
        jq = lax.broadcasted_iota(jnp.int32, s.shape, 1) % n_new
        _softmax_step_t(jnp.where(jk <= jq, s, NEG_INF), v, m_sc, l_sc, acc_sc)
        lam = _lambda_of(lam_ref, lam_init)
        o = acc_sc[...] / l_sc[...]
        for h in range(hw // B_VDIM):
            sl = slice(h * B_VDIM, (h + 1) * B_VDIM)
            oh = jnp.transpose(o[sl, :])
            r0 = h * 2 * n_new
            of = oh[r0:r0 + n_new, :] - lam * oh[r0 + n_new:r0 + 2 * n_new, :]
            o_ref[:, sl] = _sub_ln(of, g_ref[...], lam_init).astype(o_ref.dtype)


def _attn_b_sample(wq, pool, page_base, kvn, page_table, lam_p, subln_g, lam_init):
    s, rows, hw = wq.shape
    n_new = kvn.shape[1]
    n_pages = page_table.shape[1]
    pps = B_PAGES_PER_STEP
    assert n_pages % pps == 0 and rows == LANES
    kernel = functools.partial(_attn_b_sample_kernel, lam_init=lam_init, n_new=n_new,
                               n_per_step=pps)
    grid_spec = pltpu.PrefetchScalarGridSpec(
        num_scalar_prefetch=1,
        grid=(s, n_pages // pps + 1),
        in_specs=[pl.BlockSpec((None, rows, hw), lambda b, p, pt: (b, 0, 0))]
        + _page_specs(pps, n_pages, page_base, (None, PAGE_SIZE, 2 * hw))
        + [
            pl.BlockSpec((None, n_new, 2 * hw), lambda b, p, pt: (b, 0, 0)),
            pl.BlockSpec((4, HEAD_DIM), lambda *_: (0, 0)),
            pl.BlockSpec((1, B_VDIM), lambda *_: (0, 0)),
        ],
        out_specs=pl.BlockSpec((None, n_new, hw), lambda b, p, pt: (b, 0, 0)),
        scratch_shapes=[pltpu.VMEM((1, rows), F32), pltpu.VMEM((1, rows), F32),
                        pltpu.VMEM((hw, rows), F32)],
    )
    return pl.pallas_call(
        kernel,
        out_shape=jax.ShapeDtypeStruct((s, n_new, hw), BF16),
        grid_spec=grid_spec,
        compiler_params=_cparams(("arbitrary", "arbitrary")),
        name="diff_attn_sample",
    )(page_table.reshape(-1), wq, *([pool] * pps), kvn, lam_p, subln_g.reshape(1, B_VDIM))


def _plan_b():
    hw = B_HEADS * B_VDIM
    half = hw // 2
    return [
        (0, half, half, QK_SCALE, ((0, 0),)),
        (half, half, half, QK_SCALE, ((0, half),)),
        (hw, half, half, 1.0, ((1, 0), (2, 0))),
        (hw + half, half, half, 1.0, ((1, half), (2, half))),
        (2 * hw, half, 0, 1.0, ((1, hw), (2, hw))),
        (2 * hw + half, half, 0, 1.0, ((1, hw + half), (2, hw + half))),
    ]


def _mixer_b_prompt(x, mod3, tables, w_in, lam_p, subln_g, lam_init, bsz, seq_len, tm):
    hw = B_HEADS * B_VDIM
    outs = [(hw, BF16, None), (2 * hw, BF16, None), (2 * hw, F32, None)]
    q, kvb, kvf = _inproj(x, mod3, tables, w_in, _plan_b(), outs, tm, seq_len)
    o = _attn_b_prompt(q.reshape(bsz, seq_len, hw), kvb.reshape(bsz, seq_len, 2 * hw),
                       lam_p, subln_g, lam_init, min(512, seq_len))
    kv = kvf.reshape(bsz * seq_len // PAGE_SIZE, PAGE_SIZE, 2, B_HEADS, B_VDIM)
    return o.reshape(bsz * seq_len, hw), kv


def _mixer_b_sample(x, mod3, tables, w_in, lam_p, subln_g, lam_init, pools, j, page_table,
                    n_seq, n_new):
    hw = B_HEADS * B_VDIM
    outs = [(hw, BF16, None), (2 * hw, F32, None), (2 * hw, F32, None)]
    q, _, kvf = _inproj(x, mod3, tables, w_in, _plan_b(), outs, x.shape[0], n_new)
    q4 = q.reshape(n_seq, n_new, B_HEADS, 2, HEAD_DIM)
    eye_h = jnp.eye(B_HEADS, dtype=BF16)
    eye_2 = jnp.eye(2, dtype=BF16)
    wq = (q4.transpose(0, 2, 3, 1, 4)[:, :, :, :, None, None, :]
          * eye_h[None, :, None, None, :, None, None] * eye_2[None, None, :, None, None, :, None])
    wq = wq.reshape(n_seq, B_HEADS * 2 * n_new, hw)
    kvn = kvf.reshape(n_seq, n_new, 2 * hw)
    n_phys = pools.shape[1]
    o = _attn_b_sample(wq, pools.reshape(pools.shape[0] * n_phys, PAGE_SIZE, 2 * hw), j * n_phys,
                       kvn, page_table, lam_p, subln_g, lam_init)
    return o.reshape(n_seq * n_new, hw), kvf.reshape(n_seq, n_new, 2, B_HEADS, B_VDIM)


def _sortable_key(score):
    bits = lax.bitcast_convert_type(score + 0.0, jnp.int32)
    return bits ^ ((bits >> 31) & jnp.int32(0x7FFFFFFF))


def _kth_largest_key(count_ge, shape, k):
    t0 = jnp.full(shape, INT_MIN, jnp.int32)
    t0 = jnp.where(count_ge(jnp.zeros(shape, jnp.int32)) >= k, jnp.zeros(shape, jnp.int32), t0)

    def body(it, t):
        cand = t + jnp.left_shift(jnp.int32(1), 30 - it)
        return jnp.where(count_ge(cand) >= k, cand, t)

    return lax.fori_loop(0, 31, body, t0)


def _select_prompt_kernel(qi_ref, kiw_all_ref, kiw_q_ref, sel_ref, sc_ref, *, topk):
    tq = qi_ref.shape[0]
    l = kiw_all_ref.shape[0]
    i = pl.program_id(1)
    tk = min(1024, l)
    n_pairs = qi_ref.shape[1] // LANES
    wi = kiw_q_ref[:, IDX_DIM:IDX_DIM + IDX_HEADS] * IDX_W_SCALE
    qs = jnp.concatenate([_pair_stack(qi_ref[:, p * LANES:(p + 1) * LANES])
                          for p in range(n_pairs)], axis=0)
    for c in range(l // tk):
        kic = kiw_all_ref[c * tk:(c + 1) * tk, :]
        lane = lax.broadcasted_iota(jnp.int32, kic.shape, 1)
        ki2 = jnp.where(lane < IDX_DIM, kic, pltpu.roll(kic, IDX_DIM, 1)).astype(BF16)
        r = jnp.maximum(_dot_nt(qs, ki2), 0.0)
        sc = jnp.zeros((tq, tk), F32)
        for h in range(IDX_HEADS):
            sc = sc + r[h * tq:(h + 1) * tq, :] * wi[:, h:h + 1]
        row = lax.broadcasted_iota(jnp.int32, sc.shape, 0) + i * tq
        col = lax.broadcasted_iota(jnp.int32, sc.shape, 1) + c * tk
        sc_ref[:, c * tk:(c + 1) * tk] = _sortable_key(jnp.where(col <= row, sc, NEG_INF))

    def count_ge(t):
        return jnp.sum((sc_ref[...] >= t).astype(jnp.int32), axis=1, keepdims=True)

    thr = _kth_largest_key(count_ge, (tq, 1), topk)
    n_gt = jnp.sum((sc_ref[...] > thr).astype(jnp.int32), axis=1, keepdims=True)
    need = (topk - n_gt).astype(F32)
    tri = (lax.broadcasted_iota(jnp.int32, (LANES, LANES), 0)
           < lax.broadcasted_iota(jnp.int32, (LANES, LANES), 1)).astype(BF16)
    offs = jnp.zeros((tq, 1), F32)
    row = lax.broadcasted_iota(jnp.int32, (tq, LANES), 0) + i * tq
    lane = lax.broadcasted_iota(jnp.int32, (tq, LANES), 1)
    for c in range(l // LANES):
        keys = sc_ref[:, c * LANES:(c + 1) * LANES]
        causal = lane + c * LANES <= row
        eq = jnp.logical_and(keys == thr, causal)
        e = jnp.where(eq, 1.0, 0.0)
        rank = _dot(e.astype(BF16), tri) + offs
        sel = jnp.logical_or(keys > thr, jnp.logical_and(eq, rank < need))
        sel_ref[:, c * LANES:(c + 1) * LANES] = jnp.where(sel, 1.0, 0.0).astype(sel_ref.dtype)
        offs = offs + jnp.sum(e, axis=1, keepdims=True)


def _select_prompt(qi, kiw, topk):
    b, l, w = qi.shape
    tq = PAGE_SIZE
    return pl.pallas_call(
        functools.partial(_select_prompt_kernel, topk=topk),
        out_shape=jax.ShapeDtypeStruct((b, l, l), BF16),
        grid=(b, l // tq),
        in_specs=[
            pl.BlockSpec((None, tq, w), lambda b_, i: (b_, i, 0)),
            pl.BlockSpec((None, l, LANES), lambda b_, i: (b_, 0, 0)),
            pl.BlockSpec((None, tq, LANES), lambda b_, i: (b_, i, 0)),
        ],
        out_specs=pl.BlockSpec((None, tq, l), lambda b_, i: (b_, i, 0)),
        scratch_shapes=[pltpu.VMEM((tq, l), jnp.int32)],
        compiler_params=_cparams(("arbitrary", "arbitrary")),
        name="indexer_select_prompt",
    )(qi, kiw, kiw)


def _attn_c_prompt_kernel(q_ref, k_ref, v_ref, sel_ref, o_ref, m_sc, l_sc, acc_sc):
    tq = q_ref.shape[0]
    n_q_chunks = q_ref.shape[1] // LANES
    per_kv = n_q_chunks // (k_ref.shape[1] // LANES)
    qi = pl.program_id(1)
    ki = pl.program_id(2)
    tk = k_ref.shape[0]
    last = (qi * tq + tq - 1) // tk

    @pl.when(ki == 0)
    def _():
        m_sc[...] = jnp.full(m_sc.shape, NEG_INF, F32)
        l_sc[...] = jnp.zeros(l_sc.shape, F32)
        acc_sc[...] = jnp.zeros(acc_sc.shape, F32)

    @pl.when(ki <= last)
    def _():
        sel = sel_ref[...].astype(F32)
        rows_kv = 2 * per_kv * tq
        selx = jnp.concatenate([sel] * (2 * per_kv), axis=0) > 0.5
        for a in range(k_ref.shape[1] // LANES):
            qs = jnp.concatenate(
                [_pair_stack(q_ref[:, (a * per_kv + e) * LANES:(a * per_kv + e + 1) * LANES])
                 for e in range(per_kv)], axis=0)
            s = _dot_nt(qs, k_ref[:, a * LANES:(a + 1) * LANES])
            s = jnp.where(selx, s, NEG_INF)
            rs = slice(a * rows_kv, (a + 1) * rows_kv)
            m_old = m_sc[rs, :]
            m_new = jnp.maximum(m_old, jnp.max(s, axis=1, keepdims=True))
            m_use = jnp.where(m_new == NEG_INF, 0.0, m_new)
            alpha = jnp.exp(m_old - m_use)
            p = jnp.exp(s - m_use)
            l_sc[rs, :] = alpha * l_sc[rs, :] + jnp.sum(p, axis=1, keepdims=True)
            acc_sc[rs, :] = alpha * acc_sc[rs, :] + _dot(p.astype(BF16),
                                                         v_ref[:, a * LANES:(a + 1) * LANES])
            m_sc[rs, :] = m_new

    @pl.when(ki == pl.num_programs(2) - 1)
    def _():
        o = acc_sc[...] / l_sc[...]
        for c in range(n_q_chunks):
            top = o[(2 * c) * tq:(2 * c + 1) * tq, :]
            bot = o[(2 * c + 1) * tq:(2 * c + 2) * tq, :]
            o_ref[:, c * LANES:(c + 1) * LANES] = _pair_merge(top, bot).astype(o_ref.dtype)


def _attn_c_prompt(q, kd, vd, sel, tk):
    b, l, qw = q.shape
    tq = PAGE_SIZE
    kw = kd.shape[2]
    nk = l // tk
    rows = 2 * (qw // LANES) * tq
    last = lambda i: (i * tq + tq - 1) // tk
    return pl.pallas_call(
        _attn_c_prompt_kernel,
        out_shape=jax.ShapeDtypeStruct(q.shape, BF16),
        grid=(b, l // tq, nk),
        in_specs=[
            pl.BlockSpec((None, tq, qw), lambda b_, i, k: (b_, i, 0)),
            pl.BlockSpec((None, tk, kw), lambda b_, i, k: (b_, jnp.minimum(k, last(i)), 0)),
            pl.BlockSpec((None, tk, kw), lambda b_, i, k: (b_, jnp.minimum(k, last(i)), 0)),
            pl.BlockSpec((None, tq, tk), lambda b_, i, k: (b_, i, jnp.minimum(k, last(i)))),
        ],
        out_specs=pl.BlockSpec((None, tq, qw), lambda b_, i, k: (b_, i, 0)),
        scratch_shapes=[pltpu.VMEM((rows, 1), F32), pltpu.VMEM((rows, 1), F32),
                        pltpu.VMEM((rows, LANES), F32)],
        compiler_params=_cparams(("arbitrary",) * 3),
        name="sparse_attn_prompt",
    )(q, kd, vd, sel)


KEY_NEG_INF = -2139095041
SELECT_ROW_CHUNK = 512


def _select_prompt_t_kernel(qi_ref, kiw_all_ref, kiw_q_ref, sel_ref, key_sc, *, topk):
    tq = qi_ref.shape[0]
    l = kiw_all_ref.shape[0]
    i = pl.program_id(1)
    rc = min(SELECT_ROW_CHUNK, l)
    n_pairs = qi_ref.shape[1] // LANES
    q_hi = i * tq + tq - 1
    wi_t = jnp.transpose(kiw_q_ref[...])[IDX_DIM:IDX_DIM + IDX_HEADS, :] * IDX_W_SCALE
    qs = jnp.concatenate([_pair_stack(qi_ref[:, p * LANES:(p + 1) * LANES])
                          for p in range(n_pairs)], axis=0)
    qpos = lax.broadcasted_iota(jnp.int32, (rc, tq), 1) + i * tq
    krow = lax.broadcasted_iota(jnp.int32, (rc, tq), 0)
    for c in range(l // rc):
        rows = slice(c * rc, (c + 1) * rc)

        @pl.when(c * rc <= q_hi)
        def _(c=c, rows=rows):
            kic = kiw_all_ref[rows, :]
            lane = lax.broadcasted_iota(jnp.int32, kic.shape, 1)
            ki2 = jnp.where(lane < IDX_DIM, kic, pltpu.roll(kic, IDX_DIM, 1)).astype(BF16)
            r = jnp.maximum(_dot_nt(ki2, qs), 0.0)
            sc = jnp.zeros((rc, tq), F32)
            for h in range(IDX_HEADS):
                sc = sc + r[:, h * tq:(h + 1) * tq] * wi_t[h:h + 1, :]
            key_sc[rows, :] = _sortable_key(jnp.where(krow + c * rc <= qpos, sc, NEG_INF))

        @pl.when(c * rc > q_hi)
        def _(rows=rows):
            key_sc[rows, :] = jnp.full((rc, tq), KEY_NEG_INF, jnp.int32)

    n_scan = q_hi // rc + 1

    def count(pred):
        def body(c, acc):
            k = key_sc[pl.ds(pl.multiple_of(c * rc, rc), rc), :]
            return acc + jnp.sum(jnp.where(pred(k), 1, 0).reshape(rc // 8, 8, tq), axis=0)
        acc = lax.fori_loop(0, n_scan, body, jnp.zeros((8, tq), jnp.int32))
        return jnp.sum(acc, axis=0, keepdims=True)

    def count_ge(t):
        return count(lambda k: k >= t) + jnp.where(t <= KEY_NEG_INF, l - n_scan * rc, 0)

    thr = _kth_largest_key(count_ge, (1, tq), topk)
    need = (topk - count(lambda k: k > thr)).astype(F32)
    tri = (lax.broadcasted_iota(jnp.int32, (LANES, LANES), 1)
           < lax.broadcasted_iota(jnp.int32, (LANES, LANES), 0)).astype(BF16)
    qpos = lax.broadcasted_iota(jnp.int32, (LANES, tq), 1) + i * tq
    krow = lax.broadcasted_iota(jnp.int32, (LANES, tq), 0)
    offs = jnp.zeros((1, tq), F32)
    for c in range(l // LANES):
        keys = key_sc[c * LANES:(c + 1) * LANES, :]
        eq = jnp.logical_and(keys == thr, krow + c * LANES <= qpos)
        e = jnp.where(eq, 1.0, 0.0)
        rank = _dot(tri, e.astype(BF16)) + offs
        sel = jnp.logical_or(keys > thr, jnp.logical_and(eq, rank < need))
        sel_ref[c * LANES:(c + 1) * LANES, :] = jnp.where(sel, 1.0, 0.0).astype(sel_ref.dtype)
        offs = offs + jnp.sum(e, axis=0, keepdims=True)


def _select_prompt_t(qi, kiw, topk):
    b, l, w = qi.shape
    tq = PAGE_SIZE
    return pl.pallas_call(
        functools.partial(_select_prompt_t_kernel, topk=topk),
        out_shape=jax.ShapeDtypeStruct((b, l, l), BF16),
        grid=(b, l // tq),
        in_specs=[
            pl.BlockSpec((None, tq, w), lambda b_, i: (b_, i, 0)),
            pl.BlockSpec((None, l, LANES), lambda b_, i: (b_, 0, 0)),
            pl.BlockSpec((None, tq, LANES), lambda b_, i: (b_, i, 0)),
        ],
        out_specs=pl.BlockSpec((None, l, tq), lambda b_, i: (b_, 0, i)),
        scratch_shapes=[pltpu.VMEM((l, tq), jnp.int32)],
        compiler_params=_cparams(("arbitrary", "arbitrary")),
        name="indexer_select_prompt",
    )(qi, kiw, kiw)


def _attn_c_prompt_t_kernel(qi_tab, ki_tab, q_ref, k_ref, v_ref, sel_ref, o_ref,
                            m_sc, l_sc, acc_sc):
    tq = q_ref.shape[0]
    tk = k_ref.shape[0]
    n_kv = k_ref.shape[1] // LANES
    per_kv = q_ref.shape[1] // LANES // n_kv
    cols_kv = 2 * per_kv * tq
    t = pl.program_id(1)
    qi = qi_tab[t]
    ki = ki_tab[t]

    @pl.when(ki == 0)
    def _():
        _init_softmax_state(m_sc, l_sc, acc_sc)

    sel = sel_ref[...].astype(F32)
    selx = jnp.concatenate([sel] * (2 * per_kv), axis=1) > 0.5
    for a in range(n_kv):
        qs = jnp.concatenate(
            [_pair_stack(q_ref[:, (a * per_kv + e) * LANES:(a * per_kv + e + 1) * LANES])
             for e in range(per_kv)], axis=0)
        s = jnp.where(selx, _dot_nt(k_ref[:, a * LANES:(a + 1) * LANES], qs), NEG_INF)
        _softmax_step_t(s, v_ref[:, a * LANES:(a + 1) * LANES], m_sc, l_sc, acc_sc,
                        may_be_empty=True, cols=slice(a * cols_kv, (a + 1) * cols_kv))

    @pl.when(ki == (qi * tq + tq - 1) // tk)
    def _():
        o = acc_sc[...] / l_sc[...]
        for c in range(n_kv * per_kv):
            lo = o[0:HEAD_DIM, (2 * c) * tq:(2 * c + 1) * tq]
            hi = o[0:HEAD_DIM, (2 * c + 1) * tq:(2 * c + 2) * tq]
            o_ref[:, c * LANES:(c + 1) * LANES] = jnp.transpose(
                jnp.concatenate([lo, hi], axis=0)).astype(o_ref.dtype)


def _attn_c_prompt_t(q, kd, vd, sel_t, tk):
    b, l, qw = q.shape
    tq = PAGE_SIZE
    kw = kd.shape[2]
    cols = 2 * (qw // LANES) * tq
    pairs = [(i, k) for i in range(l // tq) for k in range((i * tq + tq - 1) // tk + 1)]
    qi_tab = jnp.array([p[0] for p in pairs], jnp.int32)
    ki_tab = jnp.array([p[1] for p in pairs], jnp.int32)
    grid_spec = pltpu.PrefetchScalarGridSpec(
        num_scalar_prefetch=2,
        grid=(b, len(pairs)),
        in_specs=[
            pl.BlockSpec((None, tq, qw), lambda b_, t, qt, kt: (b_, qt[t], 0)),
            pl.BlockSpec((None, tk, kw), lambda b_, t, qt, kt: (b_, kt[t], 0)),
            pl.BlockSpec((None, tk, kw), lambda b_, t, qt, kt: (b_, kt[t], 0)),
            pl.BlockSpec((None, tk, tq), lambda b_, t, qt, kt: (b_, kt[t], qt[t])),
        ],
        out_specs=pl.BlockSpec((None, tq, qw), lambda b_, t, qt, kt: (b_, qt[t], 0)),
        scratch_shapes=[pltpu.VMEM((1, cols), F32), pltpu.VMEM((1, cols), F32),
                        pltpu.VMEM((LANES, cols), F32)],
    )
    return pl.pallas_call(
        _attn_c_prompt_t_kernel,
        out_shape=jax.ShapeDtypeStruct(q.shape, BF16),
        grid_spec=grid_spec,
        compiler_params=_cparams(("arbitrary", "arbitrary")),
        name="sparse_attn_prompt",
    )(qi_tab, ki_tab, q, kd, vd, sel_t)


C_PAGES_PER_STEP = 8


def _select_sample_kernel(pt_ref, qs_ref, wi_ref, *refs, topk, n_new, n_per_step):
    page_refs = refs[:n_per_step]
    kin_ref, sel_ref, key_sc = refs[n_per_step:]
    p_id = pl.program_id(1)
    n_steps = pl.num_programs(1) - 1
    n_pages = key_sc.shape[0] - 1
    wi = wi_ref[...] * IDX_W_SCALE

    def scores(ki):
        r = jnp.maximum(_dot_nt(qs_ref[...], ki), 0.0)
        sc = jnp.zeros((n_new, PAGE_SIZE), F32)
        for h in range(IDX_HEADS):
            sc = sc + r[h * n_new:(h + 1) * n_new, :] * wi[:, h:h + 1]
        return sc

    @pl.when(p_id < n_steps)
    def _():
        for e, page_ref in enumerate(page_refs):
            key_sc[p_id * n_per_step + e] = _sortable_key(scores(page_ref[...].astype(BF16)))

    @pl.when(p_id == n_steps)
    def _():
        for e in range(1, n_per_step):
            sel_ref[n_pages + e] = jnp.zeros((n_new, PAGE_SIZE), F32)
        sc = scores(_pad_rows(kin_ref[...], PAGE_SIZE).astype(BF16))
        jq = lax.broadcasted_iota(jnp.int32, sc.shape, 0)
        jk = lax.broadcasted_iota(jnp.int32, sc.shape, 1)
        key_sc[n_pages] = _sortable_key(jnp.where(jk <= jq, sc, NEG_INF))

        def count(pred):
            c = jnp.where(pred(key_sc[...]), 1, 0)
            return jnp.sum(jnp.sum(c, axis=0), axis=1, keepdims=True)

        thr = _kth_largest_key(lambda t: count(lambda k: k >= t[None]), (n_new, 1), topk)
        need = (topk - count(lambda k: k > thr[None])).astype(F32)
        tri = (lax.broadcasted_iota(jnp.int32, (LANES, LANES), 0)
               < lax.broadcasted_iota(jnp.int32, (LANES, LANES), 1)).astype(BF16)

        def body(p, offs):
            keys = key_sc[p]
            valid = jnp.logical_or(p < n_pages, jk <= jq)
            eq = jnp.logical_and(keys == thr, valid)
            e = jnp.where(eq, 1.0, 0.0)
            rank = _dot(e.astype(BF16), tri) + offs
            sel = jnp.logical_or(keys > thr, jnp.logical_and(eq, rank < need))
            sel_ref[p] = jnp.where(sel, 1.0, 0.0)
            return offs + jnp.sum(e, axis=1, keepdims=True)

        lax.fori_loop(0, n_pages + 1, body, jnp.zeros((n_new, 1), F32))


def _select_sample(qs, wi, pool_idx, page_base, ki_new, page_table, topk):
    s, rows, _ = qs.shape
    n_new = ki_new.shape[1]
    n_pages = page_table.shape[1]
    pps = C_PAGES_PER_STEP
    assert n_pages % pps == 0
    kernel = functools.partial(_select_sample_kernel, topk=topk, n_new=n_new, n_per_step=pps)
    grid_spec = pltpu.PrefetchScalarGridSpec(
        num_scalar_prefetch=1,
        grid=(s, n_pages // pps + 1),
        in_specs=[
            pl.BlockSpec((None, rows, IDX_DIM), lambda b, p, pt: (b, 0, 0)),
            pl.BlockSpec((None, n_new, IDX_HEADS), lambda b, p, pt: (b, 0, 0)),
        ] + _page_specs(pps, n_pages, page_base, (None, PAGE_SIZE, IDX_DIM)) + [
            pl.BlockSpec((None, n_new, IDX_DIM), lambda b, p, pt: (b, 0, 0)),
        ],
        out_specs=pl.BlockSpec((None, n_pages + pps, n_new, PAGE_SIZE),
                               lambda b, p, pt: (b, 0, 0, 0)),
        scratch_shapes=[pltpu.VMEM((n_pages + 1, n_new, PAGE_SIZE), jnp.int32)],
    )
    return pl.pallas_call(
        kernel,
        out_shape=jax.ShapeDtypeStruct((s, n_pages + pps, n_new, PAGE_SIZE), F32),
        grid_spec=grid_spec,
        compiler_params=_cparams(("arbitrary", "arbitrary")),
        name="indexer_select_sample",
    )(page_table.reshape(-1), qs, wi, *([pool_idx] * pps), ki_new)


def _attn_c_sample_kernel(pt_ref, wq_ref, *refs, n_new, n_per_step):
    page_refs = refs[:n_per_step]
    kvn_ref, sel_ref, o_ref, m_sc, l_sc, acc_sc = refs[n_per_step:]
    kw = wq_ref.shape[1]
    n_rep = wq_ref.shape[0] // n_new
    p_id = pl.program_id(1)
    n_steps = pl.num_programs(1) - 1

    @pl.when(p_id == 0)
    def _():
        _init_softmax_state(m_sc, l_sc, acc_sc)

    def update(k, v, sel):
        sel = jnp.concatenate([sel] * n_rep, axis=0) > 0.5
        s = jnp.where(sel, _dot_nt(wq_ref[...], k), NEG_INF)
        m_old = m_sc[...]
        m_new = jnp.maximum(m_old, jnp.max(s, axis=1, keepdims=True))
        m_use = jnp.where(m_new == NEG_INF, 0.0, m_new)
        alpha = jnp.exp(m_old - m_use)
        p = jnp.exp(s - m_use)
        l_sc[...] = alpha * l_sc[...] + jnp.sum(p, axis=1, keepdims=True)
        acc_sc[...] = alpha * acc_sc[...] + _dot(p.astype(BF16), v)
        m_sc[...] = m_new

    @pl.when(p_id < n_steps)
    def _():
        update(jnp.concatenate([r[:, 0:kw].astype(BF16) for r in page_refs], axis=0),
               jnp.concatenate([r[:, kw:2 * kw].astype(BF16) for r in page_refs], axis=0),
               jnp.concatenate([sel_ref[e] for e in range(n_per_step)], axis=1))

    @pl.when(p_id == n_steps)
    def _():
        update(_pad_rows(kvn_ref[:, 0:kw], PAGE_SIZE).astype(BF16),
               _pad_rows(kvn_ref[:, kw:2 * kw], PAGE_SIZE).astype(BF16), sel_ref[0])
        o_ref[...] = acc_sc[...] / l_sc[...]


def _attn_c_sample(wq, pool, page_base, kvn, sel, page_table):
    s, rows, kw = wq.shape
    n_new = kvn.shape[1]
    n_pages = page_table.shape[1]
    pps = C_PAGES_PER_STEP
    grid_spec = pltpu.PrefetchScalarGridSpec(
        num_scalar_prefetch=1,
        grid=(s, n_pages // pps + 1),
        in_specs=[pl.BlockSpec((None, rows, kw), lambda b, p, pt: (b, 0, 0))]
        + _page_specs(pps, n_pages, page_base, (None, PAGE_SIZE, 2 * kw))
        + [
            pl.BlockSpec((None, n_new, 2 * kw), lambda b, p, pt: (b, 0, 0)),
            pl.BlockSpec((None, pps, n_new, PAGE_SIZE), lambda b, p, pt: (b, p, 0, 0)),
        ],
        out_specs=pl.BlockSpec((None, rows, kw), lambda b, p, pt: (b, 0, 0)),
        scratch_shapes=[pltpu.VMEM((rows, 1), F32), pltpu.VMEM((rows, 1), F32),
                        pltpu.VMEM((rows, kw), F32)],
    )
    return pl.pallas_call(
        functools.partial(_attn_c_sample_kernel, n_new=n_new, n_per_step=pps),
        out_shape=jax.ShapeDtypeStruct((s, rows, kw), F32),
        grid_spec=grid_spec,
        compiler_params=_cparams(("arbitrary", "arbitrary")),
        name="sparse_attn_sample",
    )(page_table.reshape(-1), wq, *([pool] * pps), kvn, sel)


C_QW = C_HEADS * HEAD_DIM
C_KW = C_KV_HEADS * HEAD_DIM
C_IW = IDX_HEADS * IDX_DIM


def _prep_c_weights(w):
    q, k, v, qi, ki, wi = jnp.split(w, [C_QW, C_QW + C_KW, C_QW + 2 * C_KW,
                                        C_QW + 2 * C_KW + C_IW, C_QW + 2 * C_KW + C_IW + IDX_DIM],
                                    axis=1)
    dup = lambda a: jnp.repeat(a.reshape(a.shape[0], C_KV_HEADS, 1, HEAD_DIM), 2, axis=2).reshape(
        a.shape[0], 2 * C_KW)
    pad = jnp.zeros((w.shape[0], LANES - IDX_DIM - IDX_HEADS), w.dtype)
    return jnp.concatenate([q, k, v, qi, ki, wi, pad, dup(k), dup(v)], axis=1)


def _plan_c():
    o_kv = C_QW
    o_qi = C_QW + 2 * C_KW
    o_ki = o_qi + C_IW
    o_kd = o_ki + LANES
    o_vd = o_kd + 2 * C_KW
    return [
        (0, 512, 512, QK_SCALE, ((0, 0),)),
        (512, 512, 512, QK_SCALE, ((0, 512),)),
        (o_kv, 2 * C_KW, C_KW, 1.0, ((1, 0),)),
        (o_qi, C_IW, C_IW, 1.0, ((2, 0),)),
        (o_ki, LANES, IDX_DIM, 1.0, ((3, 0),)),
        (o_kd, 2 * C_KW, 2 * C_KW, 1.0, ((4, 0),)),
        (o_vd, 2 * C_KW, 0, 1.0, ((5, 0),)),
    ]


def _mixer_c_prompt(x, mod3, tables, w_prep, bsz, seq_len, tm):
    outs = [(C_QW, BF16, None), (2 * C_KW, F32, None), (C_IW, BF16, None), (LANES, F32, None),
            (2 * C_KW, BF16, None), (2 * C_KW, BF16, None)]
    q, kvf, qi, kiw, kd, vd = _inproj(x, mod3, tables, w_prep, _plan_c(), outs, tm, seq_len)
    topk = min(C_TOPK_MAX, seq_len // 4)
    r3 = lambda a: a.reshape(bsz, seq_len, a.shape[-1])
    sel_t = _select_prompt_t(r3(qi), r3(kiw), topk)
    o = _attn_c_prompt_t(r3(q), r3(kd), r3(vd), sel_t, min(512, seq_len))
    n_pg = bsz * seq_len // PAGE_SIZE
    kv = kvf.reshape(n_pg, PAGE_SIZE, 2, C_KV_HEADS, HEAD_DIM)
    ki = kiw[:, :IDX_DIM].reshape(n_pg, PAGE_SIZE, IDX_DIM)
    return o.reshape(bsz * seq_len, C_QW), kv, ki


def _mixer_c_sample(x, mod3, tables, w_prep, pools_kv, pools_idx, j, page_table, n_seq, n_new):
    n_phys = pools_kv.shape[1]
    pool_kv = pools_kv.reshape(pools_kv.shape[0] * n_phys, PAGE_SIZE, 2 * C_KW)
    pool_idx = pools_idx.reshape(pools_idx.shape[0] * n_phys, PAGE_SIZE, IDX_DIM)
    outs = [(C_QW, BF16, None), (2 * C_KW, F32, None), (C_IW, BF16, None), (LANES, F32, None),
            (2 * C_KW, BF16, None), (2 * C_KW, BF16, None)]
    q, kvf, qi, kiw, _, _ = _inproj(x, mod3, tables, w_prep, _plan_c(), outs, x.shape[0], n_new)
    n_past = page_table.shape[1] * PAGE_SIZE
    topk = min(C_TOPK_MAX, (n_past + n_new) // 4)
    qs = qi.reshape(n_seq, n_new, IDX_HEADS, IDX_DIM).transpose(0, 2, 1, 3).reshape(
        n_seq, IDX_HEADS * n_new, IDX_DIM)
    kiw3 = kiw.reshape(n_seq, n_new, LANES)
    ki_new = kiw3[:, :, :IDX_DIM]
    wi = kiw3[:, :, IDX_DIM:IDX_DIM + IDX_HEADS]
    sel = _select_sample(qs, wi, pool_idx, j * n_phys, ki_new, page_table, topk)
    q4 = q.reshape(n_seq, n_new, C_HEADS, HEAD_DIM).transpose(0, 2, 1, 3)
    kv_of = jnp.arange(C_HEADS) // (C_HEADS // C_KV_HEADS)
    onehot = (kv_of[:, None] == jnp.arange(C_KV_HEADS)[None, :]).astype(BF16)
    wq = (q4[:, :, :, None, :] * onehot[None, :, None, :, None]).reshape(
        n_seq, C_HEADS * n_new, C_KW)
    kvn = kvf.reshape(n_seq, n_new, 2 * C_KW)
    acc = _attn_c_sample(wq, pool_kv, j * n_phys, kvn, sel, page_table)
    acc = acc.reshape(n_seq, C_HEADS, n_new, C_KV_HEADS, HEAD_DIM)
    o = jnp.einsum("shjad,ha->sjhd", acc, onehot.astype(F32))
    o = o.reshape(n_seq * n_new, C_QW).astype(BF16)
    return (o, kvf.reshape(n_seq, n_new, 2, C_KV_HEADS, HEAD_DIM),
            ki_new.reshape(n_seq, n_new, IDX_DIM))


def kernel(x_prompt, x_sample, state_a_kv_w128, state_a_kv_w512, state_a_kv_w2048, cache_b_kv,
           cache_c_kv, cache_c_idx, page_table, c_prompt, c_sample, w_mod, b_mod, ln_g, ln_b,
           w_ffn_in, w_ffn_out, a_w_in, a_w_out, b_w_in, b_w_out, b_lambda, b_subln_g, c_w_in,
           c_w_out):
    bsz, seq_len, d = x_prompt.shape
    n_seq, n_new, _ = x_sample.shape
    n_past = page_table.shape[1] * PAGE_SIZE
    a_states = (state_a_kv_w128, state_a_kv_w512, state_a_kv_w2048)
    tm = min(512, seq_len)
    rs = n_seq * n_new

    n_c = bsz + n_seq
    n_c_pad = -(-n_c // 8) * 8
    c_all = jnp.concatenate([c_prompt, c_sample, jnp.zeros((n_c_pad - n_c, d), F32)], axis=0)
    mod_all = _modulation(c_all, w_mod, b_mod)

    tables_p = _rope_tables(jnp.arange(seq_len, dtype=jnp.int32))
    tables_s = tuple(jnp.tile(t, (n_seq, 1)) for t in
                     _rope_tables(n_past + jnp.arange(n_new, dtype=jnp.int32)))

    xp = x_prompt.reshape(bsz * seq_len, d)
    xs = x_sample.reshape(rs, d)
    new_a_p = [[] for _ in A_GROUPS]
    a_flat = [s.reshape(s.shape[0] * n_seq, s.shape[2], 2 * A_WIDTH) for s in a_states]
    a_bufs = [_shift_state(s, n_new) for s in a_flat]
    b_p, b_s, ckv_p, ckv_s, cidx_p, cidx_s = [], [], [], [], [], []

    for i in range(DEPTH):
        kind, j = i % N_MIXERS, i // N_MIXERS
        mod_p = mod_all[i, :bsz][:, None, :]
        mod_s = jnp.repeat(mod_all[i, bsz:bsz + n_seq], n_new, axis=0)[None]
        if kind == 0:
            w_in = a_w_in[j].astype(BF16)
            w_out = a_w_out[j].astype(BF16)
            op, bufs_p = _mixer_a_prompt(xp, mod_p, tables_p, w_in, bsz, seq_len, tm)
            os_, a_bufs = _mixer_a_sample(xs, mod_s, tables_s, w_in, a_flat, a_bufs, j,
                                          n_seq, n_new)
            for g in range(len(A_GROUPS)):
                new_a_p[g].append(bufs_p[g])
        elif kind == 1:
            w_in = b_w_in[j].astype(BF16)
            w_out = b_w_out[j].astype(BF16)
            lam_init = 0.8 - 0.6 * math.exp(-0.3 * i)
            op, kv_p = _mixer_b_prompt(xp, mod_p, tables_p, w_in, b_lambda[j], b_subln_g[j],
                                       lam_init, bsz, seq_len, tm)
            os_, kv_s = _mixer_b_sample(xs, mod_s, tables_s, w_in, b_lambda[j], b_subln_g[j],
                                        lam_init, cache_b_kv, j, page_table, n_seq, n_new)
            b_p.append(kv_p)
            b_s.append(kv_s)
        else:
            w_in = _prep_c_weights(c_w_in[j]).astype(BF16)
            w_out = c_w_out[j].astype(BF16)
            op, kv_p, ki_p = _mixer_c_prompt(xp, mod_p, tables_p, w_in, bsz, seq_len, tm)
            os_, kv_s, ki_s = _mixer_c_sample(xs, mod_s, tables_s, w_in, cache_c_kv, cache_c_idx,
                                              j, page_table, n_seq, n_new)
            ckv_p.append(kv_p)
            ckv_s.append(kv_s)
            cidx_p.append(ki_p)
            cidx_s.append(ki_s)
        w1 = w_ffn_in[i].astype(BF16)
        w2 = w_ffn_out[i].astype(BF16)
        xp = _post(xp, op, mod_p, w_out, ln_g[i], ln_b[i], w1, w2, tm, seq_len)
        xs = _post(xs, os_, mod_s, w_out, ln_g[i], ln_b[i], w1, w2, rs, n_new)

    y_prompt = xp.reshape(bsz, seq_len, d)
    y_sample = xs.reshape(n_seq, n_new, d)
    stack = lambda xs: xs[0][None] if len(xs) == 1 else jnp.stack(xs)
    a_p = [stack(a) for a in new_a_p]
    a_s = [b.reshape(s.shape) for b, s in zip(a_bufs, a_states)]
    return (y_prompt, y_sample, a_p[0], a_s[0], a_p[1], a_s[1], a_p[2], a_s[2],
            stack(b_p), stack(b_s), stack(ckv_p), stack(ckv_s), stack(cidx_p), stack(cidx_s))
```

```python
import functools
import math

import jax
import jax.numpy as jnp
from jax import lax
from jax.experimental import pallas as pl
from jax.experimental.pallas import tpu as pltpu

F32 = jnp.float32
BF16 = jnp.bfloat16

DEPTH = 4
N_MIXERS = 3
D_MODEL = 1024
HEAD_DIM = 64
PAGE_SIZE = 128
ROPE_THETA = 500000.0
LN_EPS = 1e-5
DEEPNORM_ALPHA = (2 * DEPTH) ** 0.25
QK_SCALE = HEAD_DIM ** -0.5

A_GROUPS = ((128, 1), (512, 4), (2048, 16))
A_HEADS = 8
A_WIDTH = A_HEADS * HEAD_DIM
B_HEADS = 8
B_VDIM = 2 * HEAD_DIM
C_HEADS = 16
C_KV_HEADS = 4
IDX_HEADS = 8
IDX_DIM = 64
C_TOPK_MAX = 256
IDX_W_SCALE = IDX_HEADS ** -0.5 * IDX_DIM ** -0.5
FFN_HIDDEN = 2816

LANES = 128
VMEM_LIMIT_BYTES = 56 * 1024 * 1024
NEG_INF = float("-inf")
INT_MIN = -(2 ** 31)


def _cparams(semantics, vmem=VMEM_LIMIT_BYTES):
    return pltpu.CompilerParams(dimension_semantics=semantics, vmem_limit_bytes=vmem)


def _dot(a, b):
    return jnp.dot(a, b, preferred_element_type=F32)


def _dot_nt(a, b):
    return lax.dot_general(a, b, (((1,), (1,)), ((), ())), preferred_element_type=F32)


def _dot_tn(a, b):
    return lax.dot_general(a, b, (((0,), (0,)), ((), ())), preferred_element_type=F32)


def _resident(shape):
    nd = len(shape)
    return pl.BlockSpec(shape, lambda *_: (0,) * nd, pipeline_mode=pl.Buffered(1))


def _split_bf16(a):
    hi = a.astype(BF16)
    lo = (a - hi.astype(F32)).astype(BF16)
    return hi, lo


def _mod_kernel(c_ref, w_ref, b_ref, o_ref):
    c = c_ref[...]
    a_hi, a_lo = _split_bf16(jax.nn.silu(c))
    w_hi, w_lo = _split_bf16(w_ref[...])
    o_ref[...] = _dot(a_hi, w_hi) + _dot(a_hi, w_lo) + _dot(a_lo, w_hi) + b_ref[...]


def _modulation(c_all, w_mod, b_mod):
    n, d = c_all.shape
    depth, _, n6 = w_mod.shape
    tn = 1536
    return pl.pallas_call(
        _mod_kernel,
        out_shape=jax.ShapeDtypeStruct((depth, n, n6), F32),
        grid=(depth, n6 // tn),
        in_specs=[
            pl.BlockSpec((n, d), lambda i, j: (0, 0)),
            pl.BlockSpec((None, d, tn), lambda i, j: (i, 0, j)),
            pl.BlockSpec((None, 1, tn), lambda i, j: (i, 0, j)),
        ],
        out_specs=pl.BlockSpec((None, n, tn), lambda i, j: (i, 0, j)),
        compiler_params=_cparams(("arbitrary", "arbitrary")),
        name="adaln_modulation",
    )(c_all, w_mod, b_mod.reshape(depth, 1, n6))


def _rope_tables(pos):
    rot = HEAD_DIM // 4
    half = rot // 2
    inv = ROPE_THETA ** (-jnp.arange(0, rot, 2, dtype=F32) / rot)
    ang = pos.astype(F32)[:, None] * inv[None, :]
    cos, sin = jnp.cos(ang), jnp.sin(ang)
    n = pos.shape[0]
    ones = jnp.ones((n, HEAD_DIM - rot), F32)
    zeros = jnp.zeros((n, HEAD_DIM - rot), F32)
    zh = jnp.zeros((n, half), F32)
    c = jnp.concatenate([cos, cos, ones], axis=1)
    sa = jnp.concatenate([-sin, zh, zeros], axis=1)
    sb = jnp.concatenate([zh, sin, zeros], axis=1)
    rep = LANES // HEAD_DIM
    return jnp.tile(c, (1, rep)), jnp.tile(sa, (1, rep)), jnp.tile(sb, (1, rep))


INPROJ_SEG = 512


def _rows_to_lanes(y, o_ref, col, dil, stage_ref):
    tm, wd = y.shape
    slot = o_ref.shape[1] // dil
    for c in range(wd // LANES):
        stage_ref[c] = y[:, c * LANES:(c + 1) * LANES]
    for r in range(dil):
        for c in range(wd // LANES):
            lo = r * slot + col + c * LANES
            o_ref[:, lo:lo + LANES] = stage_ref[c, pl.ds(r, tm // dil, stride=dil), :].astype(
                o_ref.dtype)


def _lanes_to_rows(x_ref, dil, stage_ref):
    n, tot = x_ref.shape
    wd = tot // dil
    for r in range(dil):
        for c in range(wd // LANES):
            lo = r * wd + c * LANES
            stage_ref[c, pl.ds(r, n, stride=dil), :] = x_ref[:, lo:lo + LANES].astype(F32)
    return jnp.concatenate([stage_ref[c] for c in range(wd // LANES)], axis=1)


def _inproj_kernel(x_ref, mod_ref, cos_ref, sa_ref, sb_ref, w_ref, *refs,
                   plan, keeps, tiles_per_seq):
    out_refs, stage_ref = refs[:-1], refs[-1]
    d = x_ref.shape[1]
    tm = x_ref.shape[0]
    x = x_ref[...]
    h = (x * (1.0 + mod_ref[:, d:2 * d]) + mod_ref[:, 0:d]).astype(BF16)
    cos, sa, sb = cos_ref[...], sa_ref[...], sb_ref[...]
    half = HEAD_DIM // 8
    t_in = pl.program_id(0) % tiles_per_seq
    for c0, wd, rope_cols, scale, dests in plan:
        y = _dot(h, w_ref[:, c0:c0 + wd])
        if rope_cols:
            reps = wd // LANES
            tile = lambda t: jnp.concatenate([t] * reps, axis=1) if reps > 1 else t
            yr = (y * tile(cos) + pltpu.roll(y, wd - half, 1) * tile(sa)
                  + pltpu.roll(y, half, 1) * tile(sb))
            if rope_cols < wd:
                lane = lax.broadcasted_iota(jnp.int32, y.shape, 1)
                y = jnp.where(lane < rope_cols, yr, y)
            else:
                y = yr
        if scale != 1.0:
            y = y * scale
        for oi, col in dests:
            o_ref = out_refs[oi]
            keep = keeps[oi]
            if keep is None:
                o_ref[:, col:col + wd] = y.astype(o_ref.dtype)
            elif isinstance(keep, tuple):
                if keep[1] == 1:
                    o_ref[:, col:col + wd] = y.astype(o_ref.dtype)
                else:
                    _rows_to_lanes(y, o_ref, col, keep[1], stage_ref)
            elif keep >= tm:
                first = tiles_per_seq - keep // tm

                @pl.when(t_in >= first)
                def _(o_ref=o_ref, y=y, col=col, wd=wd):
                    o_ref[:, col:col + wd] = y.astype(o_ref.dtype)
            else:
                @pl.when(t_in == tiles_per_seq - 1)
                def _(o_ref=o_ref, y=y, col=col, wd=wd, keep=keep):
                    o_ref[:, col:col + wd] = y[tm - keep:, :].astype(o_ref.dtype)


def _inproj(x, mod3, tables, w, plan, outs, tm, seq_len):
    r, d = x.shape
    n_tiles = r // tm
    tps = seq_len // tm if mod3.shape[1] == 1 else 1
    tbl_tiles = tables[0].shape[0] // tm
    rm = mod3.shape[1]
    out_shapes, out_specs, keeps = [], [], []
    for cols, dtype, keep in outs:
        keeps.append(keep)
        if keep is None:
            out_shapes.append(jax.ShapeDtypeStruct((r, cols), dtype))
            out_specs.append(pl.BlockSpec((tm, cols), lambda t: (t, 0)))
        elif isinstance(keep, tuple):
            dil = keep[1]
            assert tm % (16 * dil) == 0
            out_shapes.append(jax.ShapeDtypeStruct((r // dil, dil * cols), dtype))
            out_specs.append(pl.BlockSpec((tm // dil, dil * cols), lambda t: (t, 0)))
        elif keep >= tm:
            assert keep % tm == 0 and mod3.shape[1] == 1
            first = tps - keep // tm
            out_shapes.append(jax.ShapeDtypeStruct((r // seq_len, keep, cols), dtype))
            out_specs.append(pl.BlockSpec(
                (None, tm, cols),
                lambda t, first=first: (t // tps, jnp.maximum(t % tps - first, 0), 0)))
        else:
            assert tm % keep == 0 and keep % 8 == 0 and mod3.shape[1] == 1
            out_shapes.append(jax.ShapeDtypeStruct((r // seq_len, keep, cols), dtype))
            out_specs.append(pl.BlockSpec((None, keep, cols), lambda t: (t // tps, 0, 0)))
    tbl_spec = pl.BlockSpec((tm, LANES), lambda t: (t % tbl_tiles, 0))
    kernel = functools.partial(_inproj_kernel, plan=tuple(plan), keeps=tuple(keeps),
                               tiles_per_seq=tps)
    return pl.pallas_call(
        kernel,
        out_shape=out_shapes,
        grid=(n_tiles,),
        in_specs=[
            pl.BlockSpec((tm, d), lambda t: (t, 0)),
            pl.BlockSpec((None, rm, 2 * d), lambda t: (t // tps, 0, 0)),
            tbl_spec, tbl_spec, tbl_spec,
            _resident(w.shape),
        ],
        out_specs=out_specs,
        scratch_shapes=[pltpu.VMEM((INPROJ_SEG // LANES, tm, LANES), F32)],
        compiler_params=_cparams(("arbitrary",)),
        name="mixer_in_proj",
    )(x, mod3, *tables, w)


def _layer_norm(x, g, b):
    mu = jnp.mean(x, axis=-1, keepdims=True)
    xc = x - mu
    var = jnp.mean(xc * xc, axis=-1, keepdims=True)
    return xc * lax.rsqrt(var + LN_EPS) * g + b


FFN_CHUNK = 256


def _post_kernel(x_ref, o_ref, mod_ref, wo_ref, lng_ref, lnb_ref, w1_ref, w2_ref,
                 y_ref, acc_ref):
    d = x_ref.shape[1]
    f = w2_ref.shape[0]
    x = x_ref[...]
    g1 = mod_ref[:, 2 * d:3 * d]
    sh2 = mod_ref[:, 3 * d:4 * d]
    sc2 = mod_ref[:, 4 * d:5 * d]
    g2 = mod_ref[:, 5 * d:6 * d]
    m = _dot(o_ref[...], wo_ref[...])
    x1 = _layer_norm(DEEPNORM_ALPHA * x + (1.0 + g1) * m, lng_ref[0:1, :], lnb_ref[0:1, :])
    h = (x1 * (1.0 + sc2) + sh2).astype(BF16)
    for c in range(f // FFN_CHUNK):
        c0 = c * FFN_CHUNK
        g = _dot(h, w1_ref[:, c0:c0 + FFN_CHUNK])
        u = _dot(h, w1_ref[:, f + c0:f + c0 + FFN_CHUNK])
        part = _dot((jax.nn.silu(g) * u).astype(BF16), w2_ref[c0:c0 + FFN_CHUNK, :])
        if c == 0:
            acc_ref[...] = part
        else:
            acc_ref[...] += part
    y_ref[...] = _layer_norm(DEEPNORM_ALPHA * x1 + (1.0 + g2) * acc_ref[...],
                             lng_ref[1:2, :], lnb_ref[1:2, :])


def _post(x, o, mod3, wo, lng, lnb, w1, w2, tm, seq_len):
    r, d = x.shape
    wo_in = o.shape[1]
    rm = mod3.shape[1]
    tps = seq_len // tm if rm == 1 else 1
    return pl.pallas_call(
        _post_kernel,
        out_shape=jax.ShapeDtypeStruct((r, d), F32),
        grid=(r // tm,),
        in_specs=[
            pl.BlockSpec((tm, d), lambda t: (t, 0)),
            pl.BlockSpec((tm, wo_in), lambda t: (t, 0)),
            pl.BlockSpec((None, rm, 6 * d), lambda t: (t // tps, 0, 0)),
            _resident(wo.shape), _resident(lng.shape), _resident(lnb.shape),
            _resident(w1.shape), _resident(w2.shape),
        ],
        out_specs=pl.BlockSpec((tm, d), lambda t: (t, 0)),
        scratch_shapes=[pltpu.VMEM((tm, d), F32)],
        compiler_params=_cparams(("arbitrary",)),
        name="out_proj_ffn",
    )(x, o, mod3, wo, lng, lnb, w1, w2)


def _pair_stack(qp):
    lane = lax.broadcasted_iota(jnp.int32, qp.shape, 1)
    lo = lane < HEAD_DIM
    zero = jnp.zeros_like(qp)
    return jnp.concatenate([jnp.where(lo, qp, zero), jnp.where(lo, zero, qp)], axis=0)


def _pair_merge(top, bot):
    lane = lax.broadcasted_iota(jnp.int32, top.shape, 1)
    return jnp.where(lane < HEAD_DIM, top, bot)


def _attn_a_prompt_kernel(q_ref, kvc_ref, kvp_ref, o_ref, lse_ref):
    tq = q_ref.shape[0]
    width = q_ref.shape[1]
    i = pl.program_id(2)
    krow = lax.broadcasted_iota(jnp.int32, (tq, 2 * tq), 0)
    qcol = lax.broadcasted_iota(jnp.int32, (tq, 2 * tq), 1) % tq
    mask_c = krow <= qcol
    mask_p = jnp.logical_and(krow >= qcol, i > 0)
    for j in range(width // LANES):
        sl = slice(j * LANES, (j + 1) * LANES)
        vsl = slice(width + j * LANES, width + (j + 1) * LANES)
        qs = _pair_stack(q_ref[:, sl])
        s_c = jnp.where(mask_c, _dot_nt(kvc_ref[:, sl], qs), NEG_INF)
        s_p = jnp.where(mask_p, _dot_nt(kvp_ref[:, sl], qs), NEG_INF)
        m = jnp.maximum(jnp.max(s_c, axis=0, keepdims=True), jnp.max(s_p, axis=0, keepdims=True))
        p_c = jnp.exp(s_c - m)
        p_p = jnp.exp(s_p - m)
        l = jnp.sum(p_c, axis=0, keepdims=True) + jnp.sum(p_p, axis=0, keepdims=True)
        ov = (_dot_tn(kvc_ref[:, vsl], p_c.astype(BF16))
              + _dot_tn(kvp_ref[:, vsl], p_p.astype(BF16))) / l
        lse = jnp.broadcast_to(m + jnp.log(l), (HEAD_DIM, 2 * tq))
        o_ref[:, sl] = jnp.transpose(
            jnp.concatenate([ov[:HEAD_DIM, :tq], ov[HEAD_DIM:, tq:]], axis=0))
        lse_ref[:, sl] = jnp.transpose(jnp.concatenate([lse[:, :tq], lse[:, tq:]], axis=0))


def _attn_a_prompt(q, kv, dil):
    b, ld, tot = q.shape
    width = tot // dil
    tq = PAGE_SIZE
    nb = ld // tq
    o, lse = pl.pallas_call(
        _attn_a_prompt_kernel,
        out_shape=[jax.ShapeDtypeStruct(q.shape, F32), jax.ShapeDtypeStruct(q.shape, F32)],
        grid=(b, dil, nb),
        in_specs=[
            pl.BlockSpec((None, tq, width), lambda b_, r, i: (b_, i, r)),
            pl.BlockSpec((None, tq, 2 * width), lambda b_, r, i: (b_, i, r)),
            pl.BlockSpec((None, tq, 2 * width), lambda b_, r, i: (b_, jnp.maximum(i - 1, 0), r)),
        ],
        out_specs=[pl.BlockSpec((None, tq, width), lambda b_, r, i: (b_, i, r))] * 2,
        compiler_params=_cparams(("arbitrary",) * 3),
        name="dilated_attn_prompt",
    )(q, kv, kv)
    return o, lse


A_SAMPLE_KEY_CHUNK = 512


def _attn_a_sample_kernel(wq_ref, k_ref, v_ref, kn_ref, vn_ref, o_ref, lse_ref,
                          m_sc, l_sc, acc_sc, *, dil, window, w_buf):
    kc, nh, _ = k_ref.shape
    cols = wq_ref.shape[0]
    n_new = cols // nh
    c = pl.program_id(1)

    def flat(ref):
        return ref[...].reshape(-1, HEAD_DIM).astype(BF16)

    def step(k_ref_, v_ref_, key0):
        nk = k_ref_.shape[0]
        s = _dot_nt(flat(k_ref_), wq_ref[...]).reshape(nk, nh, cols)
        key = lax.broadcasted_iota(jnp.int32, s.shape, 0) + key0
        head = lax.broadcasted_iota(jnp.int32, s.shape, 1)
        col = lax.broadcasted_iota(jnp.int32, s.shape, 2)
        off = w_buf + col % n_new - key
        admit = jnp.logical_and(head == col // n_new,
                                jnp.logical_and(jnp.logical_and(off >= 0, off <= window),
                                                (off & (dil - 1)) == 0))
        s = jnp.where(admit, s, NEG_INF).reshape(nk * nh, cols)
        _softmax_step_t(s, flat(v_ref_), m_sc, l_sc, acc_sc, may_be_empty=True)

    @pl.when(c == 0)
    def _():
        _init_softmax_state(m_sc, l_sc, acc_sc)

    step(k_ref, v_ref, c * kc)

    @pl.when(c == pl.num_programs(1) - 1)
    def _():
        step(kn_ref, vn_ref, w_buf)
        l = l_sc[...]
        o_ref[...] = jnp.transpose(acc_sc[...] / l)
        lse_ref[...] = m_sc[...] + jnp.log(l)


def _attn_a_sample(wq, state, seq_base, kvn, dil, window):
    s, cols, _ = wq.shape
    w_buf, nh = state.shape[1], state.shape[3]
    n_new = kvn.shape[1]
    kc = min(A_SAMPLE_KEY_CHUNK, w_buf)
    assert w_buf % kc == 0
    blk = lambda rows: (None, rows, None, nh, HEAD_DIM)
    kernel = functools.partial(_attn_a_sample_kernel, dil=dil, window=window, w_buf=w_buf)
    return pl.pallas_call(
        kernel,
        out_shape=[jax.ShapeDtypeStruct((s, cols, HEAD_DIM), F32),
                   jax.ShapeDtypeStruct((s, 1, cols), F32)],
        grid=(s, w_buf // kc),
        in_specs=[
            pl.BlockSpec((None, cols, HEAD_DIM), lambda b, c: (b, 0, 0)),
            pl.BlockSpec(blk(kc), lambda b, c: (seq_base + b, c, 0, 0, 0)),
            pl.BlockSpec(blk(kc), lambda b, c: (seq_base + b, c, 1, 0, 0)),
            pl.BlockSpec(blk(n_new), lambda b, c: (b, 0, 0, 0, 0)),
            pl.BlockSpec(blk(n_new), lambda b, c: (b, 0, 1, 0, 0)),
        ],
        out_specs=[pl.BlockSpec((None, cols, HEAD_DIM), lambda b, c: (b, 0, 0)),
                   pl.BlockSpec((None, 1, cols), lambda b, c: (b, 0, 0))],
        scratch_shapes=[pltpu.VMEM((1, cols), F32), pltpu.VMEM((1, cols), F32),
                        pltpu.VMEM((HEAD_DIM, cols), F32)],
        compiler_params=_cparams(("arbitrary", "arbitrary")),
        name="dilated_attn_sample",
    )(wq, state, state, kvn, kvn)


def _shift_state_kernel(src_ref, dst_ref, zero_ref, sems, *, n_new):
    n, w = src_ref.shape[0], src_ref.shape[1]
    zero_ref[...] = jnp.zeros(zero_ref.shape, zero_ref.dtype)
    body = pltpu.make_async_copy(src_ref.at[:, pl.ds(n_new, w - n_new)],
                                 dst_ref.at[:, pl.ds(0, w - n_new)], sems.at[n])
    tails = [pltpu.make_async_copy(zero_ref, dst_ref.at[i, pl.ds(w - n_new, n_new)], sems.at[i])
             for i in range(n)]
    body.start()
    for cp in tails:
        cp.start()
    for cp in tails:
        cp.wait()
    body.wait()


def _shift_state(state, n_new):
    n = state.shape[0]
    return pl.pallas_call(
        functools.partial(_shift_state_kernel, n_new=n_new),
        out_shape=jax.ShapeDtypeStruct(state.shape, state.dtype),
        in_specs=[pl.BlockSpec(memory_space=pl.ANY)],
        out_specs=pl.BlockSpec(memory_space=pl.ANY),
        scratch_shapes=[pltpu.VMEM((n_new,) + state.shape[2:], state.dtype),
                        pltpu.SemaphoreType.DMA((n + 1,))],
        name="window_shift",
    )(state)


def _set_tail_kernel(new_ref, buf_ref, o_ref):
    del buf_ref
    o_ref[...] = new_ref[...]


def _set_tail(buf, new_rows, seq_base):
    s, n_new = new_rows.shape[:2]
    rest = new_rows.shape[2:]
    zeros = (0,) * len(rest)
    w = buf.shape[1]
    assert w % n_new == 0
    return pl.pallas_call(
        _set_tail_kernel,
        out_shape=jax.ShapeDtypeStruct(buf.shape, buf.dtype),
        grid=(s,),
        in_specs=[pl.BlockSpec((None, n_new) + rest, lambda b: (b, 0) + zeros),
                  pl.BlockSpec(memory_space=pl.ANY)],
        out_specs=pl.BlockSpec((None, n_new) + rest,
                               lambda b: (seq_base + b, w // n_new - 1) + zeros),
        input_output_aliases={1: 0},
        compiler_params=_cparams(("arbitrary",)),
        name="window_append",
    )(new_rows, buf)


def _combine_a_kernel(*refs, dils):
    n = len(dils)
    o_refs, l_refs, out_ref, stage_ref = refs[:n], refs[n:2 * n], refs[2 * n], refs[2 * n + 1]
    load = lambda ref, dil: ref[...] if dil == 1 else _lanes_to_rows(ref, dil, stage_ref)
    lses = [load(r, dil) for r, dil in zip(l_refs, dils)]
    m = functools.reduce(jnp.maximum, lses)
    ws = [jnp.exp(l - m) for l in lses]
    num = sum(w * load(r, dil) for w, r, dil in zip(ws, o_refs, dils))
    out_ref[...] = (num / sum(ws)).astype(out_ref.dtype)


def _combine_a(os_, lses, dils, tm):
    width = os_[0].shape[1] // dils[0]
    r = os_[0].shape[0] * dils[0]
    specs = [pl.BlockSpec((tm // dil, dil * width), lambda t: (t, 0)) for dil in dils]
    return pl.pallas_call(
        functools.partial(_combine_a_kernel, dils=tuple(dils)),
        out_shape=jax.ShapeDtypeStruct((r, width), BF16),
        grid=(r // tm,),
        in_specs=specs * 2,
        out_specs=pl.BlockSpec((tm, width), lambda t: (t, 0)),
        scratch_shapes=[pltpu.VMEM((width // LANES, tm, LANES), F32)],
        compiler_params=_cparams(("arbitrary",)),
        name="dilated_combine",
    )(*os_, *lses)


def _plan_a():
    plan = []
    for g in range(len(A_GROUPS)):
        base = g * 3 * A_WIDTH
        plan.append((base, A_WIDTH, A_WIDTH, QK_SCALE, ((g, 0),)))
        plan.append((base + A_WIDTH, A_WIDTH, A_WIDTH, 1.0, ((3 + g, 0), (6 + g, 0))))
        plan.append((base + 2 * A_WIDTH, A_WIDTH, 0, 1.0, ((3 + g, A_WIDTH), (6 + g, A_WIDTH))))
    return plan


def _mixer_a_prompt(x, mod3, tables, w_in, bsz, seq_len, tm):
    keeps = [min(win, seq_len) for win, _ in A_GROUPS]
    dils = [dil for _, dil in A_GROUPS]
    outs = ([(A_WIDTH, BF16, ("dil", dil)) for dil in dils]
            + [(2 * A_WIDTH, BF16, ("dil", dil)) for dil in dils]
            + [(2 * A_WIDTH, F32, k) for k in keeps])
    res = _inproj(x, mod3, tables, w_in, _plan_a(), outs, tm, seq_len)
    os_, lses = [], []
    for g, dil in enumerate(dils):
        ld = seq_len // dil
        o, lse = _attn_a_prompt(res[g].reshape(bsz, ld, dil * A_WIDTH),
                                res[3 + g].reshape(bsz, ld, 2 * dil * A_WIDTH), dil)
        os_.append(o.reshape(bsz * ld, dil * A_WIDTH))
        lses.append(lse.reshape(bsz * ld, dil * A_WIDTH))
    o = _combine_a(os_, lses, dils, tm)
    bufs = [res[6 + g].reshape(bsz, keeps[g], 2, A_HEADS, HEAD_DIM) for g in range(3)]
    return o, bufs


def _mixer_a_sample(x, mod3, tables, w_in, states, bufs, j, n_seq, n_new):
    outs = [(A_WIDTH, F32, None)] * 3 + [(2 * A_WIDTH, F32, None)] * 3 + [(2 * A_WIDTH, F32, None)] * 3
    res = _inproj(x, mod3, tables, w_in, _plan_a(), outs, x.shape[0], n_new)
    os_, lses, new_bufs = [], [], []
    for g, (window, dil) in enumerate(A_GROUPS):
        kvn = res[3 + g].reshape(n_seq, n_new, 2, A_HEADS, HEAD_DIM)
        wq = res[g].reshape(n_seq, n_new, A_HEADS, HEAD_DIM).transpose(0, 2, 1, 3).reshape(
            n_seq, A_HEADS * n_new, HEAD_DIM).astype(BF16)
        o, lse = _attn_a_sample(wq, states[g], j * n_seq, kvn, dil, window)
        o = o.reshape(n_seq, A_HEADS, n_new, HEAD_DIM).transpose(0, 2, 1, 3)
        lse = lse.reshape(n_seq, A_HEADS, n_new, 1).transpose(0, 2, 1, 3)
        os_.append(o.reshape(n_seq * n_new, A_WIDTH))
        lses.append(jnp.broadcast_to(lse, o.shape).reshape(n_seq * n_new, A_WIDTH))
        new_bufs.append(_set_tail(bufs[g], kvn, j * n_seq))
    o = _combine_a(os_, lses, [1] * len(A_GROUPS), x.shape[0])
    return o, new_bufs


def _lambda_of(lam_ref, lam_init):
    lp = lam_ref[...]
    a = jnp.sum(lp[0:1, :] * lp[1:2, :], axis=1, keepdims=True)
    b = jnp.sum(lp[2:3, :] * lp[3:4, :], axis=1, keepdims=True)
    return jnp.exp(a) - jnp.exp(b) + lam_init


def _sub_ln(of, g, lam_init):
    ms = jnp.mean(of * of, axis=-1, keepdims=True)
    return of * lax.rsqrt(ms + LN_EPS) * g * (1.0 - lam_init)


def _init_softmax_state(m_sc, l_sc, acc_sc):
    m_sc[...] = jnp.full(m_sc.shape, NEG_INF, F32)
    l_sc[...] = jnp.zeros(l_sc.shape, F32)
    acc_sc[...] = jnp.zeros(acc_sc.shape, F32)


def _softmax_step_t(s, v, m_sc, l_sc, acc_sc, may_be_empty=False, cols=slice(None)):
    m_old = m_sc[:, cols]
    m_new = jnp.maximum(m_old, jnp.max(s, axis=0, keepdims=True))
    m_use = jnp.where(m_new == NEG_INF, 0.0, m_new) if may_be_empty else m_new
    alpha = jnp.exp(m_old - m_use)
    p = jnp.exp(s - m_use)
    l_sc[:, cols] = alpha * l_sc[:, cols] + jnp.sum(p, axis=0, keepdims=True)
    acc_sc[:, cols] = alpha * acc_sc[:, cols] + _dot_tn(v, p.astype(BF16))
    m_sc[:, cols] = m_new


def _attn_b_prompt_kernel(qi_tab, ki_tab, q_ref, k_ref, v_ref, lam_ref, g_ref, o_ref,
                          m_sc, l_sc, acc_sc, *, lam_init):
    tq = q_ref.shape[0]
    t = pl.program_id(2)
    qi = qi_tab[t]
    ki = ki_tab[t]

    @pl.when(ki == 0)
    def _():
        _init_softmax_state(m_sc, l_sc, acc_sc)

    def step(diagonal):
        qs = _pair_stack(q_ref[...])
        s = _dot_nt(k_ref[...], qs)
        if diagonal:
            kpos = lax.broadcasted_iota(jnp.int32, s.shape, 0)
            qpos = lax.broadcasted_iota(jnp.int32, s.shape, 1) % tq
            s = jnp.where(kpos <= qpos, s, NEG_INF)
        _softmax_step_t(s, v_ref[...], m_sc, l_sc, acc_sc)

    @pl.when(ki < qi)
    def _():
        step(False)

    @pl.when(ki == qi)
    def _():
        step(True)
        lam = _lambda_of(lam_ref, lam_init)
        o = acc_sc[...] / l_sc[...]
        of = o[:, :tq] - lam * o[:, tq:]
        ms = jnp.mean(of * of, axis=0, keepdims=True)
        of = jnp.transpose(of * lax.rsqrt(ms + LN_EPS))
        o_ref[...] = (of * g_ref[...] * (1.0 - lam_init)).astype(o_ref.dtype)


def _attn_b_prompt(q, kv, lam_p, subln_g, lam_init, tq):
    b, l, hw = q.shape
    nh = hw // B_VDIM
    nq = l // tq
    pairs = [(i, k) for i in range(nq) for k in range(i + 1)]
    qi_tab = jnp.array([p[0] for p in pairs], jnp.int32)
    ki_tab = jnp.array([p[1] for p in pairs], jnp.int32)
    kernel = functools.partial(_attn_b_prompt_kernel, lam_init=lam_init)
    grid_spec = pltpu.PrefetchScalarGridSpec(
        num_scalar_prefetch=2,
        grid=(b, nh, len(pairs)),
        in_specs=[
            pl.BlockSpec((None, tq, B_VDIM), lambda b_, h, t, qt, kt: (b_, qt[t], h)),
            pl.BlockSpec((None, tq, B_VDIM), lambda b_, h, t, qt, kt: (b_, kt[t], h)),
            pl.BlockSpec((None, tq, B_VDIM), lambda b_, h, t, qt, kt: (b_, kt[t], nh + h)),
            pl.BlockSpec((4, HEAD_DIM), lambda *_: (0, 0)),
            pl.BlockSpec((1, B_VDIM), lambda *_: (0, 0)),
        ],
        out_specs=pl.BlockSpec((None, tq, B_VDIM), lambda b_, h, t, qt, kt: (b_, qt[t], h)),
        scratch_shapes=[pltpu.VMEM((1, 2 * tq), F32), pltpu.VMEM((1, 2 * tq), F32),
                        pltpu.VMEM((B_VDIM, 2 * tq), F32)],
    )
    return pl.pallas_call(
        kernel,
        out_shape=jax.ShapeDtypeStruct(q.shape, BF16),
        grid_spec=grid_spec,
        compiler_params=_cparams(("arbitrary",) * 3),
        name="diff_attn_prompt",
    )(qi_tab, ki_tab, q, kv, kv, lam_p, subln_g.reshape(1, B_VDIM))


def _page_specs(n_per_step, n_pages, page_base, block, tail=None):
    n_steps = n_pages // n_per_step
    tail = (0,) * (len(block) - 1) if tail is None else tail

    def spec(e):
        def index(b, p, pt):
            page = jnp.minimum(p, n_steps - 1) * n_per_step + e
            return (page_base + pt[b * n_pages + page],) + tail
        return pl.BlockSpec(block, index)

    return [spec(e) for e in range(n_per_step)]


def _pad_rows(a, rows):
    return jnp.concatenate([a, jnp.zeros((rows - a.shape[0], a.shape[1]), a.dtype)], axis=0)


B_PAGES_PER_STEP = 4


def _attn_b_sample_kernel(pt_ref, wq_ref, *refs, lam_init, n_new, n_per_step):
    k_refs, v_refs = refs[:n_per_step], refs[n_per_step:2 * n_per_step]
    kn_ref, vn_ref, lam_ref, g_ref, o_ref, m_sc, l_sc, acc_sc = refs[2 * n_per_step:]
    nh = k_refs[0].shape[1]
    cols = wq_ref.shape[0]
    p_id = pl.program_id(1)
    n_steps = pl.num_programs(1) - 1
    same_head = (lax.broadcasted_iota(jnp.int32, (nh, cols), 0)
                 == lax.broadcasted_iota(jnp.int32, (nh, cols), 1) // (2 * n_new))

    def flat(ref_list):
        return jnp.concatenate([r[...].reshape(-1, B_VDIM) for r in ref_list], axis=0).astype(BF16)

    def scores(k, admit):
        s = _dot_nt(k, wq_ref[...])
        s3 = s.reshape(s.shape[0] // nh, nh, cols)
        return jnp.where(admit, s3, NEG_INF).reshape(s.shape)

    @pl.when(p_id == 0)
    def _():
        _init_softmax_state(m_sc, l_sc, acc_sc)

    @pl.when(p_id < n_steps)
    def _():
        _softmax_step_t(scores(flat(k_refs), same_head[None]), flat(v_refs), m_sc, l_sc, acc_sc)

    @pl.when(p_id == n_steps)
    def _():
        jk = lax.broadcasted_iota(jnp.int32, (n_new, nh, cols), 0)
        jq = lax.broadcasted_iota(jnp.int32, (n_new, nh, cols), 2) % n_new
        admit = jnp.logical_and(same_head[None], jk <= jq)
        _softmax_step_t(scores(flat([kn_ref]), admit), flat([vn_ref]), m_sc, l_sc, acc_sc)
        lam = _lambda_of(lam_ref, lam_init)
        o = jnp.transpose(acc_sc[...] / l_sc[...])
        for h in range(nh):
            r0 = h * 2 * n_new
            of = o[r0:r0 + n_new, :] - lam * o[r0 + n_new:r0 + 2 * n_new, :]
            o_ref[:, h * B_VDIM:(h + 1) * B_VDIM] = _sub_ln(of, g_ref[...], lam_init).astype(
                o_ref.dtype)


def _attn_b_sample(wq, pool, page_base, kvn, page_table, lam_p, subln_g, lam_init):
    s, cols, _ = wq.shape
    n_new, nh = kvn.shape[1], kvn.shape[3]
    n_pages = page_table.shape[1]
    pps = B_PAGES_PER_STEP
    assert n_pages % pps == 0 and cols == LANES
    kernel = functools.partial(_attn_b_sample_kernel, lam_init=lam_init, n_new=n_new,
                               n_per_step=pps)
    page_blk = (None, PAGE_SIZE, None, nh, B_VDIM)
    new_blk = (None, n_new, None, nh, B_VDIM)
    grid_spec = pltpu.PrefetchScalarGridSpec(
        num_scalar_prefetch=1,
        grid=(s, n_pages // pps + 1),
        in_specs=[pl.BlockSpec((None, cols, B_VDIM), lambda b, p, pt: (b, 0, 0))]
        + _page_specs(pps, n_pages, page_base, page_blk, (0, 0, 0, 0))
        + _page_specs(pps, n_pages, page_base, page_blk, (0, 1, 0, 0))
        + [
            pl.BlockSpec(new_blk, lambda b, p, pt: (b, 0, 0, 0, 0)),
            pl.BlockSpec(new_blk, lambda b, p, pt: (b, 0, 1, 0, 0)),
            pl.BlockSpec((4, HEAD_DIM), lambda *_: (0, 0)),
            pl.BlockSpec((1, B_VDIM), lambda *_: (0, 0)),
        ],
        out_specs=pl.BlockSpec((None, n_new, nh * B_VDIM), lambda b, p, pt: (b, 0, 0)),
        scratch_shapes=[pltpu.VMEM((1, cols), F32), pltpu.VMEM((1, cols), F32),
                        pltpu.VMEM((B_VDIM, cols), F32)],
    )
    return pl.pallas_call(
        kernel,
        out_shape=jax.ShapeDtypeStruct((s, n_new, nh * B_VDIM), BF16),
        grid_spec=grid_spec,
        compiler_params=_cparams(("arbitrary", "arbitrary")),
        name="diff_attn_sample",
    )(page_table.reshape(-1), wq, *([pool] * (2 * pps)), kvn, kvn, lam_p,
      subln_g.reshape(1, B_VDIM))


def _plan_b():
    hw = B_HEADS * B_VDIM
    half = hw // 2
    return [
        (0, half, half, QK_SCALE, ((0, 0),)),
        (half, half, half, QK_SCALE, ((0, half),)),
        (hw, half, half, 1.0, ((1, 0), (2, 0))),
        (hw + half, half, half, 1.0, ((1, half), (2, half))),
        (2 * hw, half, 0, 1.0, ((1, hw), (2, hw))),
        (2 * hw + half, half, 0, 1.0, ((1, hw + half), (2, hw + half))),
    ]


def _mixer_b_prompt(x, mod3, tables, w_in, lam_p, subln_g, lam_init, bsz, seq_len, tm):
    hw = B_HEADS * B_VDIM
    outs = [(hw, BF16, None), (2 * hw, BF16, None), (2 * hw, F32, None)]
    q, kvb, kvf = _inproj(x, mod3, tables, w_in, _plan_b(), outs, tm, seq_len)
    o = _attn_b_prompt(q.reshape(bsz, seq_len, hw), kvb.reshape(bsz, seq_len, 2 * hw),
                       lam_p, subln_g, lam_init, min(512, seq_len))
    kv = kvf.reshape(bsz * seq_len // PAGE_SIZE, PAGE_SIZE, 2, B_HEADS, B_VDIM)
    return o.reshape(bsz * seq_len, hw), kv


def _mixer_b_sample(x, mod3, tables, w_in, lam_p, subln_g, lam_init, pools, j, page_table,
                    n_seq, n_new):
    hw = B_HEADS * B_VDIM
    outs = [(hw, BF16, None), (2 * hw, F32, None), (2 * hw, F32, None)]
    q, _, kvf = _inproj(x, mod3, tables, w_in, _plan_b(), outs, x.shape[0], n_new)
    q4 = q.reshape(n_seq, n_new, B_HEADS, 2, HEAD_DIM)
    eye_2 = jnp.eye(2, dtype=BF16)
    wq = (q4.transpose(0, 2, 3, 1, 4)[:, :, :, :, None, :]
          * eye_2[None, None, :, None, :, None]).reshape(n_seq, B_HEADS * 2 * n_new, B_VDIM)
    kvn = kvf.reshape(n_seq, n_new, 2, B_HEADS, B_VDIM)
    n_phys = pools.shape[1]
    pool = pools.reshape((pools.shape[0] * n_phys,) + pools.shape[2:])
    o = _attn_b_sample(wq, pool, j * n_phys, kvn, page_table, lam_p, subln_g, lam_init)
    return o.reshape(n_seq * n_new, hw), kvn


def _sortable_key(score):
    bits = lax.bitcast_convert_type(score + 0.0, jnp.int32)
    return bits ^ ((bits >> 31) & jnp.int32(0x7FFFFFFF))


def _kth_largest_key(count_ge, shape, k):
    t0 = jnp.full(shape, INT_MIN, jnp.int32)
    t0 = jnp.where(count_ge(jnp.zeros(shape, jnp.int32)) >= k, jnp.zeros(shape, jnp.int32), t0)

    def body(it, t):
        cand = t + jnp.left_shift(jnp.int32(1), 30 - it)
        return jnp.where(count_ge(cand) >= k, cand, t)

    return lax.fori_loop(0, 31, body, t0)


def _select_prompt_kernel(qi_ref, kiw_all_ref, kiw_q_ref, sel_ref, sc_ref, *, topk):
    tq = qi_ref.shape[0]
    l = kiw_all_ref.shape[0]
    i = pl.program_id(1)
    tk = min(1024, l)
    n_pairs = qi_ref.shape[1] // LANES
    wi = kiw_q_ref[:, IDX_DIM:IDX_DIM + IDX_HEADS] * IDX_W_SCALE
    qs = jnp.concatenate([_pair_stack(qi_ref[:, p * LANES:(p + 1) * LANES])
                          for p in range(n_pairs)], axis=0)
    for c in range(l // tk):
        kic = kiw_all_ref[c * tk:(c + 1) * tk, :]
        lane = lax.broadcasted_iota(jnp.int32, kic.shape, 1)
        ki2 = jnp.where(lane < IDX_DIM, kic, pltpu.roll(kic, IDX_DIM, 1)).astype(BF16)
        r = jnp.maximum(_dot_nt(qs, ki2), 0.0)
        sc = jnp.zeros((tq, tk), F32)
        for h in range(IDX_HEADS):
            sc = sc + r[h * tq:(h + 1) * tq, :] * wi[:, h:h + 1]
        row = lax.broadcasted_iota(jnp.int32, sc.shape, 0) + i * tq
        col = lax.broadcasted_iota(jnp.int32, sc.shape, 1) + c * tk
        sc_ref[:, c * tk:(c + 1) * tk] = _sortable_key(jnp.where(col <= row, sc, NEG_INF))

    def count_ge(t):
        return jnp.sum((sc_ref[...] >= t).astype(jnp.int32), axis=1, keepdims=True)

    thr = _kth_largest_key(count_ge, (tq, 1), topk)
    n_gt = jnp.sum((sc_ref[...] > thr).astype(jnp.int32), axis=1, keepdims=True)
    need = (topk - n_gt).astype(F32)
    tri = (lax.broadcasted_iota(jnp.int32, (LANES, LANES), 0)
           < lax.broadcasted_iota(jnp.int32, (LANES, LANES), 1)).astype(BF16)
    offs = jnp.zeros((tq, 1), F32)
    row = lax.broadcasted_iota(jnp.int32, (tq, LANES), 0) + i * tq
    lane = lax.broadcasted_iota(jnp.int32, (tq, LANES), 1)
    for c in range(l // LANES):
        keys = sc_ref[:, c * LANES:(c + 1) * LANES]
        causal = lane + c * LANES <= row
        eq = jnp.logical_and(keys == thr, causal)
        e = jnp.where(eq, 1.0, 0.0)
        rank = _dot(e.astype(BF16), tri) + offs
        sel = jnp.logical_or(keys > thr, jnp.logical_and(eq, rank < need))
        sel_ref[:, c * LANES:(c + 1) * LANES] = jnp.where(sel, 1.0, 0.0).astype(sel_ref.dtype)
        offs = offs + jnp.sum(e, axis=1, keepdims=True)


def _select_prompt(qi, kiw, topk):
    b, l, w = qi.shape
    tq = PAGE_SIZE
    return pl.pallas_call(
        functools.partial(_select_prompt_kernel, topk=topk),
        out_shape=jax.ShapeDtypeStruct((b, l, l), BF16),
        grid=(b, l // tq),
        in_specs=[
            pl.BlockSpec((None, tq, w), lambda b_, i: (b_, i, 0)),
            pl.BlockSpec((None, l, LANES), lambda b_, i: (b_, 0, 0)),
            pl.BlockSpec((None, tq, LANES), lambda b_, i: (b_, i, 0)),
        ],
        out_specs=pl.BlockSpec((None, tq, l), lambda b_, i: (b_, i, 0)),
        scratch_shapes=[pltpu.VMEM((tq, l), jnp.int32)],
        compiler_params=_cparams(("arbitrary", "arbitrary")),
        name="indexer_select_prompt",
    )(qi, kiw, kiw)


def _attn_c_prompt_kernel(q_ref, k_ref, v_ref, sel_ref, o_ref, m_sc, l_sc, acc_sc):
    tq = q_ref.shape[0]
    n_q_chunks = q_ref.shape[1] // LANES
    per_kv = n_q_chunks // (k_ref.shape[1] // LANES)
    qi = pl.program_id(1)
    ki = pl.program_id(2)
    tk = k_ref.shape[0]
    last = (qi * tq + tq - 1) // tk

    @pl.when(ki == 0)
    def _():
        m_sc[...] = jnp.full(m_sc.shape, NEG_INF, F32)
        l_sc[...] = jnp.zeros(l_sc.shape, F32)
        acc_sc[...] = jnp.zeros(acc_sc.shape, F32)

    @pl.when(ki <= last)
    def _():
        sel = sel_ref[...].astype(F32)
        rows_kv = 2 * per_kv * tq
        selx = jnp.concatenate([sel] * (2 * per_kv), axis=0) > 0.5
        for a in range(k_ref.shape[1] // LANES):
            qs = jnp.concatenate(
                [_pair_stack(q_ref[:, (a * per_kv + e) * LANES:(a * per_kv + e + 1) * LANES])
                 for e in range(per_kv)], axis=0)
            s = _dot_nt(qs, k_ref[:, a * LANES:(a + 1) * LANES])
            s = jnp.where(selx, s, NEG_INF)
            rs = slice(a * rows_kv, (a + 1) * rows_kv)
            m_old = m_sc[rs, :]
            m_new = jnp.maximum(m_old, jnp.max(s, axis=1, keepdims=True))
            m_use = jnp.where(m_new == NEG_INF, 0.0, m_new)
            alpha = jnp.exp(m_old - m_use)
            p = jnp.exp(s - m_use)
            l_sc[rs, :] = alpha * l_sc[rs, :] + jnp.sum(p, axis=1, keepdims=True)
            acc_sc[rs, :] = alpha * acc_sc[rs, :] + _dot(p.astype(BF16),
                                                         v_ref[:, a * LANES:(a + 1) * LANES])
            m_sc[rs, :] = m_new

    @pl.when(ki == pl.num_programs(2) - 1)
    def _():
        o = acc_sc[...] / l_sc[...]
        for c in range(n_q_chunks):
            top = o[(2 * c) * tq:(2 * c + 1) * tq, :]
            bot = o[(2 * c + 1) * tq:(2 * c + 2) * tq, :]
            o_ref[:, c * LANES:(c + 1) * LANES] = _pair_merge(top, bot).astype(o_ref.dtype)


def _attn_c_prompt(q, kd, vd, sel, tk):
    b, l, qw = q.shape
    tq = PAGE_SIZE
    kw = kd.shape[2]
    nk = l // tk
    rows = 2 * (qw // LANES) * tq
    last = lambda i: (i * tq + tq - 1) // tk
    return pl.pallas_call(
        _attn_c_prompt_kernel,
        out_shape=jax.ShapeDtypeStruct(q.shape, BF16),
        grid=(b, l // tq, nk),
        in_specs=[
            pl.BlockSpec((None, tq, qw), lambda b_, i, k: (b_, i, 0)),
            pl.BlockSpec((None, tk, kw), lambda b_, i, k: (b_, jnp.minimum(k, last(i)), 0)),
            pl.BlockSpec((None, tk, kw), lambda b_, i, k: (b_, jnp.minimum(k, last(i)), 0)),
            pl.BlockSpec((None, tq, tk), lambda b_, i, k: (b_, i, jnp.minimum(k, last(i)))),
        ],
        out_specs=pl.BlockSpec((None, tq, qw), lambda b_, i, k: (b_, i, 0)),
        scratch_shapes=[pltpu.VMEM((rows, 1), F32), pltpu.VMEM((rows, 1), F32),
                        pltpu.VMEM((rows, LANES), F32)],
        compiler_params=_cparams(("arbitrary",) * 3),
        name="sparse_attn_prompt",
    )(q, kd, vd, sel)


KEY_NEG_INF = -2139095041
SELECT_ROW_CHUNK = 512


def _select_prompt_t_kernel(qi_ref, kiw_all_ref, kiw_q_ref, sel_ref, key_sc, *, topk):
    tq = qi_ref.shape[0]
    l = kiw_all_ref.shape[0]
    i = pl.program_id(1)
    rc = min(SELECT_ROW_CHUNK, l)
    n_pairs = qi_ref.shape[1] // LANES
    q_hi = i * tq + tq - 1
    wi_t = jnp.transpose(kiw_q_ref[...])[IDX_DIM:IDX_DIM + IDX_HEADS, :] * IDX_W_SCALE
    qs = jnp.concatenate([_pair_stack(qi_ref[:, p * LANES:(p + 1) * LANES])
                          for p in range(n_pairs)], axis=0)
    qpos = lax.broadcasted_iota(jnp.int32, (rc, tq), 1) + i * tq
    krow = lax.broadcasted_iota(jnp.int32, (rc, tq), 0)
    for c in range(l // rc):
        rows = slice(c * rc, (c + 1) * rc)

        @pl.when(c * rc <= q_hi)
        def _(c=c, rows=rows):
            kic = kiw_all_ref[rows, :]
            lane = lax.broadcasted_iota(jnp.int32, kic.shape, 1)
            ki2 = jnp.where(lane < IDX_DIM, kic, pltpu.roll(kic, IDX_DIM, 1)).astype(BF16)
            r = jnp.maximum(_dot_nt(ki2, qs), 0.0)
            sc = jnp.zeros((rc, tq), F32)
            for h in range(IDX_HEADS):
                sc = sc + r[:, h * tq:(h + 1) * tq] * wi_t[h:h + 1, :]
            key_sc[rows, :] = _sortable_key(jnp.where(krow + c * rc <= qpos, sc, NEG_INF))

        @pl.when(c * rc > q_hi)
        def _(rows=rows):
            key_sc[rows, :] = jnp.full((rc, tq), KEY_NEG_INF, jnp.int32)

    n_scan = q_hi // rc + 1

    def count(pred):
        def body(c, acc):
            k = key_sc[pl.ds(pl.multiple_of(c * rc, rc), rc), :]
            return acc + jnp.sum(jnp.where(pred(k), 1, 0).reshape(rc // 8, 8, tq), axis=0)
        acc = lax.fori_loop(0, n_scan, body, jnp.zeros((8, tq), jnp.int32))
        return jnp.sum(acc, axis=0, keepdims=True)

    def count_ge(t):
        return count(lambda k: k >= t) + jnp.where(t <= KEY_NEG_INF, l - n_scan * rc, 0)

    thr = _kth_largest_key(count_ge, (1, tq), topk)
    need = (topk - count(lambda k: k > thr)).astype(F32)
    tri = (lax.broadcasted_iota(jnp.int32, (LANES, LANES), 1)
           < lax.broadcasted_iota(jnp.int32, (LANES, LANES), 0)).astype(BF16)
    qpos = lax.broadcasted_iota(jnp.int32, (LANES, tq), 1) + i * tq
    krow = lax.broadcasted_iota(jnp.int32, (LANES, tq), 0)
    offs = jnp.zeros((1, tq), F32)
    for c in range(l // LANES):
        keys = key_sc[c * LANES:(c + 1) * LANES, :]
        eq = jnp.logical_and(keys == thr, krow + c * LANES <= qpos)
        e = jnp.where(eq, 1.0, 0.0)
        rank = _dot(tri, e.astype(BF16)) + offs
        sel = jnp.logical_or(keys > thr, jnp.logical_and(eq, rank < need))
        sel_ref[c * LANES:(c + 1) * LANES, :] = jnp.where(sel, 1.0, 0.0).astype(sel_ref.dtype)
        offs = offs + jnp.sum(e, axis=0, keepdims=True)


def _select_prompt_t(qi, kiw, topk):
    b, l, w = qi.shape
    tq = PAGE_SIZE
    return pl.pallas_call(
        functools.partial(_select_prompt_t_kernel, topk=topk),
        out_shape=jax.ShapeDtypeStruct((b, l, l), BF16),
        grid=(b, l // tq),
        in_specs=[
            pl.BlockSpec((None, tq, w), lambda b_, i: (b_, i, 0)),
            pl.BlockSpec((None, l, LANES), lambda b_, i: (b_, 0, 0)),
            pl.BlockSpec((None, tq, LANES), lambda b_, i: (b_, i, 0)),
        ],
        out_specs=pl.BlockSpec((None, l, tq), lambda b_, i: (b_, 0, i)),
        scratch_shapes=[pltpu.VMEM((l, tq), jnp.int32)],
        compiler_params=_cparams(("arbitrary", "arbitrary")),
        name="indexer_select_prompt",
    )(qi, kiw, kiw)


def _attn_c_prompt_t_kernel(qi_tab, ki_tab, q_ref, k_ref, v_ref, sel_ref, o_ref,
                            m_sc, l_sc, acc_sc):
    tq = q_ref.shape[0]
    tk = k_ref.shape[0]
    n_kv = k_ref.shape[1] // LANES
    per_kv = q_ref.shape[1] // LANES // n_kv
    cols_kv = 2 * per_kv * tq
    t = pl.program_id(1)
    qi = qi_tab[t]
    ki = ki_tab[t]

    @pl.when(ki == 0)
    def _():
        _init_softmax_state(m_sc, l_sc, acc_sc)

    sel = sel_ref[...].astype(F32)
    selx = jnp.concatenate([sel] * (2 * per_kv), axis=1) > 0.5
    for a in range(n_kv):
        qs = jnp.concatenate(
            [_pair_stack(q_ref[:, (a * per_kv + e) * LANES:(a * per_kv + e + 1) * LANES])
             for e in range(per_kv)], axis=0)
        s = jnp.where(selx, _dot_nt(k_ref[:, a * LANES:(a + 1) * LANES], qs), NEG_INF)
        _softmax_step_t(s, v_ref[:, a * LANES:(a + 1) * LANES], m_sc, l_sc, acc_sc,
                        may_be_empty=True, cols=slice(a * cols_kv, (a + 1) * cols_kv))

    @pl.when(ki == (qi * tq + tq - 1) // tk)
    def _():
        o = acc_sc[...] / l_sc[...]
        for c in range(n_kv * per_kv):
            lo = o[0:HEAD_DIM, (2 * c) * tq:(2 * c + 1) * tq]
            hi = o[0:HEAD_DIM, (2 * c + 1) * tq:(2 * c + 2) * tq]
            o_ref[:, c * LANES:(c + 1) * LANES] = jnp.transpose(
                jnp.concatenate([lo, hi], axis=0)).astype(o_ref.dtype)


def _attn_c_prompt_t(q, kd, vd, sel_t, tk):
    b, l, qw = q.shape
    tq = PAGE_SIZE
    kw = kd.shape[2]
    cols = 2 * (qw // LANES) * tq
    pairs = [(i, k) for i in range(l // tq) for k in range((i * tq + tq - 1) // tk + 1)]
    qi_tab = jnp.array([p[0] for p in pairs], jnp.int32)
    ki_tab = jnp.array([p[1] for p in pairs], jnp.int32)
    grid_spec = pltpu.PrefetchScalarGridSpec(
        num_scalar_prefetch=2,
        grid=(b, len(pairs)),
        in_specs=[
            pl.BlockSpec((None, tq, qw), lambda b_, t, qt, kt: (b_, qt[t], 0)),
            pl.BlockSpec((None, tk, kw), lambda b_, t, qt, kt: (b_, kt[t], 0)),
            pl.BlockSpec((None, tk, kw), lambda b_, t, qt, kt: (b_, kt[t], 0)),
            pl.BlockSpec((None, tk, tq), lambda b_, t, qt, kt: (b_, kt[t], qt[t])),
        ],
        out_specs=pl.BlockSpec((None, tq, qw), lambda b_, t, qt, kt: (b_, qt[t], 0)),
        scratch_shapes=[pltpu.VMEM((1, cols), F32), pltpu.VMEM((1, cols), F32),
                        pltpu.VMEM((LANES, cols), F32)],
    )
    return pl.pallas_call(
        _attn_c_prompt_t_kernel,
        out_shape=jax.ShapeDtypeStruct(q.shape, BF16),
        grid_spec=grid_spec,
        compiler_params=_cparams(("arbitrary", "arbitrary")),
        name="sparse_attn_prompt",
    )(qi_tab, ki_tab, q, kd, vd, sel_t)


C_PAGES_PER_STEP = 8


def _select_sample_kernel(pt_ref, qs_ref, wi_ref, *refs, topk, n_new, n_per_step):
    page_refs = refs[:n_per_step]
    kin_ref, sel_ref, key_sc = refs[n_per_step:]
    p_id = pl.program_id(1)
    n_steps = pl.num_programs(1) - 1
    n_pages = key_sc.shape[0] - 1
    wi = wi_ref[...] * IDX_W_SCALE

    def scores(ki):
        r = jnp.maximum(_dot_nt(qs_ref[...], ki), 0.0)
        sc = jnp.zeros((n_new, PAGE_SIZE), F32)
        for h in range(IDX_HEADS):
            sc = sc + r[h * n_new:(h + 1) * n_new, :] * wi[:, h:h + 1]
        return sc

    @pl.when(p_id < n_steps)
    def _():
        for e, page_ref in enumerate(page_refs):
            key_sc[p_id * n_per_step + e] = _sortable_key(scores(page_ref[...].astype(BF16)))

    @pl.when(p_id == n_steps)
    def _():
        for e in range(1, n_per_step):
            sel_ref[n_pages + e] = jnp.zeros((n_new, PAGE_SIZE), F32)
        sc = scores(_pad_rows(kin_ref[...], PAGE_SIZE).astype(BF16))
        jq = lax.broadcasted_iota(jnp.int32, sc.shape, 0)
        jk = lax.broadcasted_iota(jnp.int32, sc.shape, 1)
        key_sc[n_pages] = _sortable_key(jnp.where(jk <= jq, sc, NEG_INF))

        def count(pred):
            c = jnp.where(pred(key_sc[...]), 1, 0)
            return jnp.sum(jnp.sum(c, axis=0), axis=1, keepdims=True)

        thr = _kth_largest_key(lambda t: count(lambda k: k >= t[None]), (n_new, 1), topk)
        need = (topk - count(lambda k: k > thr[None])).astype(F32)
        tri = (lax.broadcasted_iota(jnp.int32, (LANES, LANES), 0)
               < lax.broadcasted_iota(jnp.int32, (LANES, LANES), 1)).astype(BF16)

        def body(p, offs):
            keys = key_sc[p]
            valid = jnp.logical_or(p < n_pages, jk <= jq)
            eq = jnp.logical_and(keys == thr, valid)
            e = jnp.where(eq, 1.0, 0.0)
            rank = _dot(e.astype(BF16), tri) + offs
            sel = jnp.logical_or(keys > thr, jnp.logical_and(eq, rank < need))
            sel_ref[p] = jnp.where(sel, 1.0, 0.0)
            return offs + jnp.sum(e, axis=1, keepdims=True)

        lax.fori_loop(0, n_pages + 1, body, jnp.zeros((n_new, 1), F32))


def _select_sample(qs, wi, pool_idx, page_base, ki_new, page_table, topk):
    s, rows, _ = qs.shape
    n_new = ki_new.shape[1]
    n_pages = page_table.shape[1]
    pps = C_PAGES_PER_STEP
    assert n_pages % pps == 0
    kernel = functools.partial(_select_sample_kernel, topk=topk, n_new=n_new, n_per_step=pps)
    grid_spec = pltpu.PrefetchScalarGridSpec(
        num_scalar_prefetch=1,
        grid=(s, n_pages // pps + 1),
        in_specs=[
            pl.BlockSpec((None, rows, IDX_DIM), lambda b, p, pt: (b, 0, 0)),
            pl.BlockSpec((None, n_new, IDX_HEADS), lambda b, p, pt: (b, 0, 0)),
        ] + _page_specs(pps, n_pages, page_base, (None, PAGE_SIZE, IDX_DIM)) + [
            pl.BlockSpec((None, n_new, IDX_DIM), lambda b, p, pt: (b, 0, 0)),
        ],
        out_specs=pl.BlockSpec((None, n_pages + pps, n_new, PAGE_SIZE),
                               lambda b, p, pt: (b, 0, 0, 0)),
        scratch_shapes=[pltpu.VMEM((n_pages + 1, n_new, PAGE_SIZE), jnp.int32)],
    )
    return pl.pallas_call(
        kernel,
        out_shape=jax.ShapeDtypeStruct((s, n_pages + pps, n_new, PAGE_SIZE), F32),
        grid_spec=grid_spec,
        compiler_params=_cparams(("arbitrary", "arbitrary")),
        name="indexer_select_sample",
    )(page_table.reshape(-1), qs, wi, *([pool_idx] * pps), ki_new)


def _attn_c_sample_kernel(pt_ref, wq_ref, *refs, n_new, n_per_step):
    page_refs = refs[:n_per_step]
    kvn_ref, sel_ref, o_ref, m_sc, l_sc, acc_sc = refs[n_per_step:]
    kw = wq_ref.shape[1]
    n_rep = wq_ref.shape[0] // n_new
    p_id = pl.program_id(1)
    n_steps = pl.num_programs(1) - 1

    @pl.when(p_id == 0)
    def _():
        _init_softmax_state(m_sc, l_sc, acc_sc)

    def update(k, v, sel):
        sel = jnp.concatenate([sel] * n_rep, axis=0) > 0.5
        s = jnp.where(sel, _dot_nt(wq_ref[...], k), NEG_INF)
        m_old = m_sc[...]
        m_new = jnp.maximum(m_old, jnp.max(s, axis=1, keepdims=True))
        m_use = jnp.where(m_new == NEG_INF, 0.0, m_new)
        alpha = jnp.exp(m_old - m_use)
        p = jnp.exp(s - m_use)
        l_sc[...] = alpha * l_sc[...] + jnp.sum(p, axis=1, keepdims=True)
        acc_sc[...] = alpha * acc_sc[...] + _dot(p.astype(BF16), v)
        m_sc[...] = m_new

    @pl.when(p_id < n_steps)
    def _():
        update(jnp.concatenate([r[:, 0:kw].astype(BF16) for r in page_refs], axis=0),
               jnp.concatenate([r[:, kw:2 * kw].astype(BF16) for r in page_refs], axis=0),
               jnp.concatenate([sel_ref[e] for e in range(n_per_step)], axis=1))

    @pl.when(p_id == n_steps)
    def _():
        update(_pad_rows(kvn_ref[:, 0:kw], PAGE_SIZE).astype(BF16),
               _pad_rows(kvn_ref[:, kw:2 * kw], PAGE_SIZE).astype(BF16), sel_ref[0])
        o_ref[...] = acc_sc[...] / l_sc[...]


def _attn_c_sample(wq, pool, page_base, kvn, sel, page_table):
    s, rows, kw = wq.shape
    n_new = kvn.shape[1]
    n_pages = page_table.shape[1]
    pps = C_PAGES_PER_STEP
    grid_spec = pltpu.PrefetchScalarGridSpec(
        num_scalar_prefetch=1,
        grid=(s, n_pages // pps + 1),
        in_specs=[pl.BlockSpec((None, rows, kw), lambda b, p, pt: (b, 0, 0))]
        + _page_specs(pps, n_pages, page_base, (None, PAGE_SIZE, 2 * kw))
        + [
            pl.BlockSpec((None, n_new, 2 * kw), lambda b, p, pt: (b, 0, 0)),
            pl.BlockSpec((None, pps, n_new, PAGE_SIZE), lambda b, p, pt: (b, p, 0, 0)),
        ],
        out_specs=pl.BlockSpec((None, rows, kw), lambda b, p, pt: (b, 0, 0)),
        scratch_shapes=[pltpu.VMEM((rows, 1), F32), pltpu.VMEM((rows, 1), F32),
                        pltpu.VMEM((rows, kw), F32)],
    )
    return pl.pallas_call(
        functools.partial(_attn_c_sample_kernel, n_new=n_new, n_per_step=pps),
        out_shape=jax.ShapeDtypeStruct((s, rows, kw), F32),
        grid_spec=grid_spec,
        compiler_params=_cparams(("arbitrary", "arbitrary")),
        name="sparse_attn_sample",
    )(page_table.reshape(-1), wq, *([pool] * pps), kvn, sel)


C_QW = C_HEADS * HEAD_DIM
C_KW = C_KV_HEADS * HEAD_DIM
C_IW = IDX_HEADS * IDX_DIM


def _prep_c_weights(w):
    q, k, v, qi, ki, wi = jnp.split(w, [C_QW, C_QW + C_KW, C_QW + 2 * C_KW,
                                        C_QW + 2 * C_KW + C_IW, C_QW + 2 * C_KW + C_IW + IDX_DIM],
                                    axis=1)
    dup = lambda a: jnp.repeat(a.reshape(a.shape[0], C_KV_HEADS, 1, HEAD_DIM), 2, axis=2).reshape(
        a.shape[0], 2 * C_KW)
    pad = jnp.zeros((w.shape[0], LANES - IDX_DIM - IDX_HEADS), w.dtype)
    return jnp.concatenate([q, k, v, qi, ki, wi, pad, dup(k), dup(v)], axis=1)


def _plan_c():
    o_kv = C_QW
    o_qi = C_QW + 2 * C_KW
    o_ki = o_qi + C_IW
    o_kd = o_ki + LANES
    o_vd = o_kd + 2 * C_KW
    return [
        (0, 512, 512, QK_SCALE, ((0, 0),)),
        (512, 512, 512, QK_SCALE, ((0, 512),)),
        (o_kv, 2 * C_KW, C_KW, 1.0, ((1, 0),)),
        (o_qi, C_IW, C_IW, 1.0, ((2, 0),)),
        (o_ki, LANES, IDX_DIM, 1.0, ((3, 0),)),
        (o_kd, 2 * C_KW, 2 * C_KW, 1.0, ((4, 0),)),
        (o_vd, 2 * C_KW, 0, 1.0, ((5, 0),)),
    ]


def _mixer_c_prompt(x, mod3, tables, w_prep, bsz, seq_len, tm):
    outs = [(C_QW, BF16, None), (2 * C_KW, F32, None), (C_IW, BF16, None), (LANES, F32, None),
            (2 * C_KW, BF16, None), (2 * C_KW, BF16, None)]
    q, kvf, qi, kiw, kd, vd = _inproj(x, mod3, tables, w_prep, _plan_c(), outs, tm, seq_len)
    topk = min(C_TOPK_MAX, seq_len // 4)
    r3 = lambda a: a.reshape(bsz, seq_len, a.shape[-1])
    sel_t = _select_prompt_t(r3(qi), r3(kiw), topk)
    o = _attn_c_prompt_t(r3(q), r3(kd), r3(vd), sel_t, min(512, seq_len))
    n_pg = bsz * seq_len // PAGE_SIZE
    kv = kvf.reshape(n_pg, PAGE_SIZE, 2, C_KV_HEADS, HEAD_DIM)
    ki = kiw[:, :IDX_DIM].reshape(n_pg, PAGE_SIZE, IDX_DIM)
    return o.reshape(bsz * seq_len, C_QW), kv, ki


def _mixer_c_sample(x, mod3, tables, w_prep, pools_kv, pools_idx, j, page_table, n_seq, n_new):
    n_phys = pools_kv.shape[1]
    pool_kv = pools_kv.reshape(pools_kv.shape[0] * n_phys, PAGE_SIZE, 2 * C_KW)
    pool_idx = pools_idx.reshape(pools_idx.shape[0] * n_phys, PAGE_SIZE, IDX_DIM)
    outs = [(C_QW, BF16, None), (2 * C_KW, F32, None), (C_IW, BF16, None), (LANES, F32, None),
            (2 * C_KW, BF16, None), (2 * C_KW, BF16, None)]
    q, kvf, qi, kiw, _, _ = _inproj(x, mod3, tables, w_prep, _plan_c(), outs, x.shape[0], n_new)
    n_past = page_table.shape[1] * PAGE_SIZE
    topk = min(C_TOPK_MAX, (n_past + n_new) // 4)
    qs = qi.reshape(n_seq, n_new, IDX_HEADS, IDX_DIM).transpose(0, 2, 1, 3).reshape(
        n_seq, IDX_HEADS * n_new, IDX_DIM)
    kiw3 = kiw.reshape(n_seq, n_new, LANES)
    ki_new = kiw3[:, :, :IDX_DIM]
    wi = kiw3[:, :, IDX_DIM:IDX_DIM + IDX_HEADS]
    sel = _select_sample(qs, wi, pool_idx, j * n_phys, ki_new, page_table, topk)
    q4 = q.reshape(n_seq, n_new, C_HEADS, HEAD_DIM).transpose(0, 2, 1, 3)
    kv_of = jnp.arange(C_HEADS) // (C_HEADS // C_KV_HEADS)
    onehot = (kv_of[:, None] == jnp.arange(C_KV_HEADS)[None, :]).astype(BF16)
    wq = (q4[:, :, :, None, :] * onehot[None, :, None, :, None]).reshape(
        n_seq, C_HEADS * n_new, C_KW)
    kvn = kvf.reshape(n_seq, n_new, 2 * C_KW)
    acc = _attn_c_sample(wq, pool_kv, j * n_phys, kvn, sel, page_table)
    acc = acc.reshape(n_seq, C_HEADS, n_new, C_KV_HEADS, HEAD_DIM)
    o = jnp.einsum("shjad,ha->sjhd", acc, onehot.astype(F32))
    o = o.reshape(n_seq * n_new, C_QW).astype(BF16)
    return (o, kvf.reshape(n_seq, n_new, 2, C_KV_HEADS, HEAD_DIM),
            ki_new.reshape(n_seq, n_new, IDX_DIM))


def kernel(x_prompt, x_sample, state_a_kv_w128, state_a_kv_w512, state_a_kv_w2048, cache_b_kv,
           cache_c_kv, cache_c_idx, page_table, c_prompt, c_sample, w_mod, b_mod, ln_g, ln_b,
           w_ffn_in, w_ffn_out, a_w_in, a_w_out, b_w_in, b_w_out, b_lambda, b_subln_g, c_w_in,
           c_w_out):
    bsz, seq_len, d = x_prompt.shape
    n_seq, n_new, _ = x_sample.shape
    n_past = page_table.shape[1] * PAGE_SIZE
    a_states = (state_a_kv_w128, state_a_kv_w512, state_a_kv_w2048)
    tm = min(512, seq_len)
    rs = n_seq * n_new

    n_c = bsz + n_seq
    n_c_pad = -(-n_c // 8) * 8
    c_all = jnp.concatenate([c_prompt, c_sample, jnp.zeros((n_c_pad - n_c, d), F32)], axis=0)
    mod_all = _modulation(c_all, w_mod, b_mod)

    tables_p = _rope_tables(jnp.arange(seq_len, dtype=jnp.int32))
    tables_s = tuple(jnp.tile(t, (n_seq, 1)) for t in
                     _rope_tables(n_past + jnp.arange(n_new, dtype=jnp.int32)))

    xp = x_prompt.reshape(bsz * seq_len, d)
    xs = x_sample.reshape(rs, d)
    new_a_p = [[] for _ in A_GROUPS]
    a_flat = [s.reshape((s.shape[0] * n_seq,) + s.shape[2:]) for s in a_states]
    a_bufs = [_shift_state(s, n_new) for s in a_flat]
    b_p, b_s, ckv_p, ckv_s, cidx_p, cidx_s = [], [], [], [], [], []

    for i in range(DEPTH):
        kind, j = i % N_MIXERS, i // N_MIXERS
        mod_p = mod_all[i, :bsz][:, None, :]
        mod_s = jnp.repeat(mod_all[i, bsz:bsz + n_seq], n_new, axis=0)[None]
        if kind == 0:
            w_in = a_w_in[j].astype(BF16)
            w_out = a_w_out[j].astype(BF16)
            op, bufs_p = _mixer_a_prompt(xp, mod_p, tables_p, w_in, bsz, seq_len, tm)
            os_, a_bufs = _mixer_a_sample(xs, mod_s, tables_s, w_in, a_flat, a_bufs, j,
                                          n_seq, n_new)
            for g in range(len(A_GROUPS)):
                new_a_p[g].append(bufs_p[g])
        elif kind == 1:
            w_in = b_w_in[j].astype(BF16)
            w_out = b_w_out[j].astype(BF16)
            lam_init = 0.8 - 0.6 * math.exp(-0.3 * i)
            op, kv_p = _mixer_b_prompt(xp, mod_p, tables_p, w_in, b_lambda[j], b_subln_g[j],
                                       lam_init, bsz, seq_len, tm)
            os_, kv_s = _mixer_b_sample(xs, mod_s, tables_s, w_in, b_lambda[j], b_subln_g[j],
                                        lam_init, cache_b_kv, j, page_table, n_seq, n_new)
            b_p.append(kv_p)
            b_s.append(kv_s)
        else:
            w_in = _prep_c_weights(c_w_in[j]).astype(BF16)
            w_out = c_w_out[j].astype(BF16)
            op, kv_p, ki_p = _mixer_c_prompt(xp, mod_p, tables_p, w_in, bsz, seq_len, tm)
            os_, kv_s, ki_s = _mixer_c_sample(xs, mod_s, tables_s, w_in, cache_c_kv, cache_c_idx,
                                              j, page_table, n_seq, n_new)
            ckv_p.append(kv_p)
            ckv_s.append(kv_s)
            cidx_p.append(ki_p)
            cidx_s.append(ki_s)
        w1 = w_ffn_in[i].astype(BF16)
        w2 = w_ffn_out[i].astype(BF16)
        xp = _post(xp, op, mod_p, w_out, ln_g[i], ln_b[i], w1, w2, tm, seq_len)
        xs = _post(xs, os_, mod_s, w_out, ln_g[i], ln_b[i], w1, w2, rs, n_new)

    y_prompt = xp.reshape(bsz, seq_len, d)
    y_sample = xs.reshape(n_seq, n_new, d)
    stack = lambda xs: xs[0][None] if len(xs) == 1 else jnp.stack(xs)
    a_p = [stack(a) for a in new_a_p]
    a_s = [b.reshape(s.shape) for b, s in zip(a_bufs, a_states)]
    return (y_prompt, y_sample, a_p[0], a_s[0], a_p[1], a_s[1], a_p[2], a_s[2],
            stack(b_p), stack(b_s), stack(ckv_p), stack(ckv_s), stack(cidx_p), stack(cidx_s))
```

```python
import functools
import math

import jax
import jax.numpy as jnp
from jax import lax
from jax.experimental import pallas as pl
from jax.experimental.pallas import tpu as pltpu

F32 = jnp.float32
BF16 = jnp.bfloat16

DEPTH = 4
N_MIXERS = 3
D_MODEL = 1024
HEAD_DIM = 64
PAGE_SIZE = 128
ROPE_THETA = 500000.0
LN_EPS = 1e-5
DEEPNORM_ALPHA = (2 * DEPTH) ** 0.25
QK_SCALE = HEAD_DIM ** -0.5

A_GROUPS = ((128, 1), (512, 4), (2048, 16))
A_HEADS = 8
A_WIDTH = A_HEADS * HEAD_DIM
B_HEADS = 8
B_VDIM = 2 * HEAD_DIM
C_HEADS = 16
C_KV_HEADS = 4
IDX_HEADS = 8
IDX_DIM = 64
C_TOPK_MAX = 256
IDX_W_SCALE = IDX_HEADS ** -0.5 * IDX_DIM ** -0.5
FFN_HIDDEN = 2816

LANES = 128
VMEM_LIMIT_BYTES = 56 * 1024 * 1024
NEG_INF = float("-inf")
INT_MIN = -(2 ** 31)


def _cparams(semantics, vmem=VMEM_LIMIT_BYTES):
    return pltpu.CompilerParams(dimension_semantics=semantics, vmem_limit_bytes=vmem)


def _dot(a, b):
    return jnp.dot(a, b, preferred_element_type=F32)


def _dot_nt(a, b):
    return lax.dot_general(a, b, (((1,), (1,)), ((), ())), preferred_element_type=F32)


def _dot_tn(a, b):
    return lax.dot_general(a, b, (((0,), (0,)), ((), ())), preferred_element_type=F32)


def _resident(shape):
    nd = len(shape)
    return pl.BlockSpec(shape, lambda *_: (0,) * nd, pipeline_mode=pl.Buffered(1))


def _split_bf16(a):
    hi = a.astype(BF16)
    lo = (a - hi.astype(F32)).astype(BF16)
    return hi, lo


def _mod_kernel(c_ref, w_ref, b_ref, o_ref):
    c = c_ref[...]
    a_hi, a_lo = _split_bf16(jax.nn.silu(c))
    w_hi, w_lo = _split_bf16(w_ref[...])
    o_ref[...] = _dot(a_hi, w_hi) + _dot(a_hi, w_lo) + _dot(a_lo, w_hi) + b_ref[...]


def _modulation(c_all, w_mod, b_mod):
    n, d = c_all.shape
    depth, _, n6 = w_mod.shape
    tn = 1536
    return pl.pallas_call(
        _mod_kernel,
        out_shape=jax.ShapeDtypeStruct((depth, n, n6), F32),
        grid=(depth, n6 // tn),
        in_specs=[
            pl.BlockSpec((n, d), lambda i, j: (0, 0)),
            pl.BlockSpec((None, d, tn), lambda i, j: (i, 0, j)),
            pl.BlockSpec((None, 1, tn), lambda i, j: (i, 0, j)),
        ],
        out_specs=pl.BlockSpec((None, n, tn), lambda i, j: (i, 0, j)),
        compiler_params=_cparams(("arbitrary", "arbitrary")),
        name="adaln_modulation",
    )(c_all, w_mod, b_mod.reshape(depth, 1, n6))


def _rope_tables(pos):
    rot = HEAD_DIM // 4
    half = rot // 2
    inv = ROPE_THETA ** (-jnp.arange(0, rot, 2, dtype=F32) / rot)
    ang = pos.astype(F32)[:, None] * inv[None, :]
    cos, sin = jnp.cos(ang), jnp.sin(ang)
    n = pos.shape[0]
    ones = jnp.ones((n, HEAD_DIM - rot), F32)
    zeros = jnp.zeros((n, HEAD_DIM - rot), F32)
    zh = jnp.zeros((n, half), F32)
    c = jnp.concatenate([cos, cos, ones], axis=1)
    sa = jnp.concatenate([-sin, zh, zeros], axis=1)
    sb = jnp.concatenate([zh, sin, zeros], axis=1)
    rep = LANES // HEAD_DIM
    return jnp.tile(c, (1, rep)), jnp.tile(sa, (1, rep)), jnp.tile(sb, (1, rep))


INPROJ_SEG = 512


def _rows_to_lanes(y, o_ref, col, dil, stage_ref):
    tm, wd = y.shape
    slot = o_ref.shape[1] // dil
    for c in range(wd // LANES):
        stage_ref[c] = y[:, c * LANES:(c + 1) * LANES]
    for r in range(dil):
        for c in range(wd // LANES):
            lo = r * slot + col + c * LANES
            o_ref[:, lo:lo + LANES] = stage_ref[c, pl.ds(r, tm // dil, stride=dil), :].astype(
                o_ref.dtype)


def _lanes_to_rows(x_ref, dil, stage_ref):
    n, tot = x_ref.shape
    wd = tot // dil
    for r in range(dil):
        for c in range(wd // LANES):
            lo = r * wd + c * LANES
            stage_ref[c, pl.ds(r, n, stride=dil), :] = x_ref[:, lo:lo + LANES].astype(F32)
    return jnp.concatenate([stage_ref[c] for c in range(wd // LANES)], axis=1)


def _inproj_kernel(x_ref, mod_ref, cos_ref, sa_ref, sb_ref, w_ref, *refs,
                   plan, keeps, tiles_per_seq):
    out_refs, stage_ref = refs[:-1], refs[-1]
    d = x_ref.shape[1]
    tm = x_ref.shape[0]
    x = x_ref[...]
    h = (x * (1.0 + mod_ref[:, d:2 * d]) + mod_ref[:, 0:d]).astype(BF16)
    cos, sa, sb = cos_ref[...], sa_ref[...], sb_ref[...]
    half = HEAD_DIM // 8
    t_in = pl.program_id(0) % tiles_per_seq
    for c0, wd, rope_cols, scale, dests in plan:
        y = _dot(h, w_ref[:, c0:c0 + wd])
        if rope_cols:
            reps = wd // LANES
            tile = lambda t: jnp.concatenate([t] * reps, axis=1) if reps > 1 else t
            yr = (y * tile(cos) + pltpu.roll(y, wd - half, 1) * tile(sa)
                  + pltpu.roll(y, half, 1) * tile(sb))
            if rope_cols < wd:
                lane = lax.broadcasted_iota(jnp.int32, y.shape, 1)
                y = jnp.where(lane < rope_cols, yr, y)
            else:
                y = yr
        if scale != 1.0:
            y = y * scale
        for oi, col in dests:
            o_ref = out_refs[oi]
            keep = keeps[oi]
            if keep is None:
                o_ref[:, col:col + wd] = y.astype(o_ref.dtype)
            elif isinstance(keep, tuple):
                if keep[1] == 1:
                    o_ref[:, col:col + wd] = y.astype(o_ref.dtype)
                else:
                    _rows_to_lanes(y, o_ref, col, keep[1], stage_ref)
            elif keep >= tm:
                first = tiles_per_seq - keep // tm

                @pl.when(t_in >= first)
                def _(o_ref=o_ref, y=y, col=col, wd=wd):
                    o_ref[:, col:col + wd] = y.astype(o_ref.dtype)
            else:
                @pl.when(t_in == tiles_per_seq - 1)
                def _(o_ref=o_ref, y=y, col=col, wd=wd, keep=keep):
                    o_ref[:, col:col + wd] = y[tm - keep:, :].astype(o_ref.dtype)


def _inproj(x, mod3, tables, w, plan, outs, tm, seq_len):
    r, d = x.shape
    n_tiles = r // tm
    tps = seq_len // tm if mod3.shape[1] == 1 else 1
    tbl_tiles = tables[0].shape[0] // tm
    rm = mod3.shape[1]
    out_shapes, out_specs, keeps = [], [], []
    for cols, dtype, keep in outs:
        keeps.append(keep)
        if keep is None:
            out_shapes.append(jax.ShapeDtypeStruct((r, cols), dtype))
            out_specs.append(pl.BlockSpec((tm, cols), lambda t: (t, 0)))
        elif isinstance(keep, tuple):
            dil = keep[1]
            assert tm % (16 * dil) == 0
            out_shapes.append(jax.ShapeDtypeStruct((r // dil, dil * cols), dtype))
            out_specs.append(pl.BlockSpec((tm // dil, dil * cols), lambda t: (t, 0)))
        elif keep >= tm:
            assert keep % tm == 0 and mod3.shape[1] == 1
            first = tps - keep // tm
            out_shapes.append(jax.ShapeDtypeStruct((r // seq_len, keep, cols), dtype))
            out_specs.append(pl.BlockSpec(
                (None, tm, cols),
                lambda t, first=first: (t // tps, jnp.maximum(t % tps - first, 0), 0)))
        else:
            assert tm % keep == 0 and keep % 8 == 0 and mod3.shape[1] == 1
            out_shapes.append(jax.ShapeDtypeStruct((r // seq_len, keep, cols), dtype))
            out_specs.append(pl.BlockSpec((None, keep, cols), lambda t: (t // tps, 0, 0)))
    tbl_spec = pl.BlockSpec((tm, LANES), lambda t: (t % tbl_tiles, 0))
    kernel = functools.partial(_inproj_kernel, plan=tuple(plan), keeps=tuple(keeps),
                               tiles_per_seq=tps)
    return pl.pallas_call(
        kernel,
        out_shape=out_shapes,
        grid=(n_tiles,),
        in_specs=[
            pl.BlockSpec((tm, d), lambda t: (t, 0)),
            pl.BlockSpec((None, rm, 2 * d), lambda t: (t // tps, 0, 0)),
            tbl_spec, tbl_spec, tbl_spec,
            _resident(w.shape),
        ],
        out_specs=out_specs,
        scratch_shapes=[pltpu.VMEM((INPROJ_SEG // LANES, tm, LANES), F32)],
        compiler_params=_cparams(("arbitrary",)),
        name="mixer_in_proj",
    )(x, mod3, *tables, w)


def _layer_norm(x, g, b):
    mu = jnp.mean(x, axis=-1, keepdims=True)
    xc = x - mu
    var = jnp.mean(xc * xc, axis=-1, keepdims=True)
    return xc * lax.rsqrt(var + LN_EPS) * g + b


FFN_CHUNK = 256


def _post_kernel(x_ref, o_ref, mod_ref, wo_ref, lng_ref, lnb_ref, w1_ref, w2_ref,
                 y_ref, acc_ref):
    d = x_ref.shape[1]
    f = w2_ref.shape[0]
    x = x_ref[...]
    g1 = mod_ref[:, 2 * d:3 * d]
    sh2 = mod_ref[:, 3 * d:4 * d]
    sc2 = mod_ref[:, 4 * d:5 * d]
    g2 = mod_ref[:, 5 * d:6 * d]
    m = _dot(o_ref[...], wo_ref[...])
    x1 = _layer_norm(DEEPNORM_ALPHA * x + (1.0 + g1) * m, lng_ref[0:1, :], lnb_ref[0:1, :])
    h = (x1 * (1.0 + sc2) + sh2).astype(BF16)
    for c in range(f // FFN_CHUNK):
        c0 = c * FFN_CHUNK
        g = _dot(h, w1_ref[:, c0:c0 + FFN_CHUNK])
        u = _dot(h, w1_ref[:, f + c0:f + c0 + FFN_CHUNK])
        part = _dot((jax.nn.silu(g) * u).astype(BF16), w2_ref[c0:c0 + FFN_CHUNK, :])
        if c == 0:
            acc_ref[...] = part
        else:
            acc_ref[...] += part
    y_ref[...] = _layer_norm(DEEPNORM_ALPHA * x1 + (1.0 + g2) * acc_ref[...],
                             lng_ref[1:2, :], lnb_ref[1:2, :])


def _post(x, o, mod3, wo, lng, lnb, w1, w2, tm, seq_len):
    r, d = x.shape
    wo_in = o.shape[1]
    rm = mod3.shape[1]
    tps = seq_len // tm if rm == 1 else 1
    return pl.pallas_call(
        _post_kernel,
        out_shape=jax.ShapeDtypeStruct((r, d), F32),
        grid=(r // tm,),
        in_specs=[
            pl.BlockSpec((tm, d), lambda t: (t, 0)),
            pl.BlockSpec((tm, wo_in), lambda t: (t, 0)),
            pl.BlockSpec((None, rm, 6 * d), lambda t: (t // tps, 0, 0)),
            _resident(wo.shape), _resident(lng.shape), _resident(lnb.shape),
            _resident(w1.shape), _resident(w2.shape),
        ],
        out_specs=pl.BlockSpec((tm, d), lambda t: (t, 0)),
        scratch_shapes=[pltpu.VMEM((tm, d), F32)],
        compiler_params=_cparams(("arbitrary",)),
        name="out_proj_ffn",
    )(x, o, mod3, wo, lng, lnb, w1, w2)


def _pair_stack(qp):
    lane = lax.broadcasted_iota(jnp.int32, qp.shape, 1)
    lo = lane < HEAD_DIM
    zero = jnp.zeros_like(qp)
    return jnp.concatenate([jnp.where(lo, qp, zero), jnp.where(lo, zero, qp)], axis=0)


def _pair_merge(top, bot):
    lane = lax.broadcasted_iota(jnp.int32, top.shape, 1)
    return jnp.where(lane < HEAD_DIM, top, bot)


def _attn_a_prompt_kernel(q_ref, kvc_ref, kvp_ref, o_ref, lse_ref):
    tq = q_ref.shape[0]
    width = q_ref.shape[1]
    i = pl.program_id(2)
    krow = lax.broadcasted_iota(jnp.int32, (tq, 2 * tq), 0)
    qcol = lax.broadcasted_iota(jnp.int32, (tq, 2 * tq), 1) % tq
    mask_c = krow <= qcol
    mask_p = jnp.logical_and(krow >= qcol, i > 0)
    for j in range(width // LANES):
        sl = slice(j * LANES, (j + 1) * LANES)
        vsl = slice(width + j * LANES, width + (j + 1) * LANES)
        qs = _pair_stack(q_ref[:, sl])
        s_c = jnp.where(mask_c, _dot_nt(kvc_ref[:, sl], qs), NEG_INF)
        s_p = jnp.where(mask_p, _dot_nt(kvp_ref[:, sl], qs), NEG_INF)
        m = jnp.maximum(jnp.max(s_c, axis=0, keepdims=True), jnp.max(s_p, axis=0, keepdims=True))
        p_c = jnp.exp(s_c - m)
        p_p = jnp.exp(s_p - m)
        l = jnp.sum(p_c, axis=0, keepdims=True) + jnp.sum(p_p, axis=0, keepdims=True)
        ov = (_dot_tn(kvc_ref[:, vsl], p_c.astype(BF16))
              + _dot_tn(kvp_ref[:, vsl], p_p.astype(BF16))) / l
        lse = jnp.broadcast_to(m + jnp.log(l), (HEAD_DIM, 2 * tq))
        o_ref[:, sl] = jnp.transpose(
            jnp.concatenate([ov[:HEAD_DIM, :tq], ov[HEAD_DIM:, tq:]], axis=0))
        lse_ref[:, sl] = jnp.transpose(jnp.concatenate([lse[:, :tq], lse[:, tq:]], axis=0))


def _attn_a_prompt(q, kv, dil):
    b, ld, tot = q.shape
    width = tot // dil
    tq = PAGE_SIZE
    nb = ld // tq
    o, lse = pl.pallas_call(
        _attn_a_prompt_kernel,
        out_shape=[jax.ShapeDtypeStruct(q.shape, F32), jax.ShapeDtypeStruct(q.shape, F32)],
        grid=(b, dil, nb),
        in_specs=[
            pl.BlockSpec((None, tq, width), lambda b_, r, i: (b_, i, r)),
            pl.BlockSpec((None, tq, 2 * width), lambda b_, r, i: (b_, i, r)),
            pl.BlockSpec((None, tq, 2 * width), lambda b_, r, i: (b_, jnp.maximum(i - 1, 0), r)),
        ],
        out_specs=[pl.BlockSpec((None, tq, width), lambda b_, r, i: (b_, i, r))] * 2,
        compiler_params=_cparams(("arbitrary",) * 3),
        name="dilated_attn_prompt",
    )(q, kv, kv)
    return o, lse


A_SAMPLE_KEY_CHUNK = 512


def _attn_a_sample_kernel(wq_ref, k_ref, v_ref, kn_ref, vn_ref, o_ref, lse_ref,
                          m_sc, l_sc, acc_sc, *, dil, window, w_buf):
    kc, nh, _ = k_ref.shape
    cols = wq_ref.shape[0]
    n_new = cols // nh
    c = pl.program_id(1)

    def flat(ref):
        return ref[...].reshape(-1, HEAD_DIM).astype(BF16)

    def step(k_ref_, v_ref_, key0):
        nk = k_ref_.shape[0]
        s = _dot_nt(flat(k_ref_), wq_ref[...]).reshape(nk, nh, cols)
        key = lax.broadcasted_iota(jnp.int32, s.shape, 0) + key0
        head = lax.broadcasted_iota(jnp.int32, s.shape, 1)
        col = lax.broadcasted_iota(jnp.int32, s.shape, 2)
        off = w_buf + col % n_new - key
        admit = jnp.logical_and(head == col // n_new,
                                jnp.logical_and(jnp.logical_and(off >= 0, off <= window),
                                                (off & (dil - 1)) == 0))
        s = jnp.where(admit, s, NEG_INF).reshape(nk * nh, cols)
        _softmax_step_t(s, flat(v_ref_), m_sc, l_sc, acc_sc, may_be_empty=True)

    @pl.when(c == 0)
    def _():
        _init_softmax_state(m_sc, l_sc, acc_sc)

    step(k_ref, v_ref, c * kc)

    @pl.when(c == pl.num_programs(1) - 1)
    def _():
        step(kn_ref, vn_ref, w_buf)
        l = l_sc[...]
        o_ref[...] = jnp.transpose(acc_sc[...] / l)
        lse_ref[...] = m_sc[...] + jnp.log(l)


def _attn_a_sample(wq, state, seq_base, kvn, dil, window):
    s, cols, _ = wq.shape
    w_buf, nh = state.shape[1], state.shape[3]
    n_new = kvn.shape[1]
    kc = min(A_SAMPLE_KEY_CHUNK, w_buf)
    assert w_buf % kc == 0
    blk = lambda rows: (None, rows, None, nh, HEAD_DIM)
    kernel = functools.partial(_attn_a_sample_kernel, dil=dil, window=window, w_buf=w_buf)
    return pl.pallas_call(
        kernel,
        out_shape=[jax.ShapeDtypeStruct((s, cols, HEAD_DIM), F32),
                   jax.ShapeDtypeStruct((s, 1, cols), F32)],
        grid=(s, w_buf // kc),
        in_specs=[
            pl.BlockSpec((None, cols, HEAD_DIM), lambda b, c: (b, 0, 0)),
            pl.BlockSpec(blk(kc), lambda b, c: (seq_base + b, c, 0, 0, 0)),
            pl.BlockSpec(blk(kc), lambda b, c: (seq_base + b, c, 1, 0, 0)),
            pl.BlockSpec(blk(n_new), lambda b, c: (b, 0, 0, 0, 0)),
            pl.BlockSpec(blk(n_new), lambda b, c: (b, 0, 1, 0, 0)),
        ],
        out_specs=[pl.BlockSpec((None, cols, HEAD_DIM), lambda b, c: (b, 0, 0)),
                   pl.BlockSpec((None, 1, cols), lambda b, c: (b, 0, 0))],
        scratch_shapes=[pltpu.VMEM((1, cols), F32), pltpu.VMEM((1, cols), F32),
                        pltpu.VMEM((HEAD_DIM, cols), F32)],
        compiler_params=_cparams(("arbitrary", "arbitrary")),
        name="dilated_attn_sample",
    )(wq, state, state, kvn, kvn)


SHIFT_ROW_CHUNK = 512


def _shift_state_kernel(cur_ref, nxt_ref, o_ref):
    rc, n_new = cur_ref.shape[0], nxt_ref.shape[0]
    c = pl.program_id(1)
    last = pl.num_programs(1) - 1
    o_ref[0:rc - n_new] = cur_ref[n_new:rc]

    @pl.when(c < last)
    def _():
        o_ref[rc - n_new:rc] = nxt_ref[...]

    @pl.when(c == last)
    def _():
        o_ref[rc - n_new:rc] = jnp.zeros(nxt_ref.shape, o_ref.dtype)


def _shift_state(state, n_new):
    n, w = state.shape[:2]
    rest = state.shape[2:]
    zeros = (0,) * len(rest)
    rc = min(SHIFT_ROW_CHUNK, w)
    assert w % rc == 0 and rc % n_new == 0
    per = rc // n_new
    return pl.pallas_call(
        _shift_state_kernel,
        out_shape=jax.ShapeDtypeStruct(state.shape, state.dtype),
        grid=(n, w // rc),
        in_specs=[
            pl.BlockSpec((None, rc) + rest, lambda b, c: (b, c) + zeros),
            pl.BlockSpec((None, n_new) + rest,
                         lambda b, c: (b, jnp.minimum((c + 1) * per, w // n_new - 1)) + zeros),
        ],
        out_specs=pl.BlockSpec((None, rc) + rest, lambda b, c: (b, c) + zeros),
        compiler_params=_cparams(("arbitrary", "arbitrary")),
        name="window_shift",
    )(state, state)


def _set_tail_kernel(new_ref, buf_ref, o_ref):
    del buf_ref
    o_ref[...] = new_ref[...]


def _set_tail(buf, new_rows, seq_base):
    s, n_new = new_rows.shape[:2]
    rest = new_rows.shape[2:]
    zeros = (0,) * len(rest)
    w = buf.shape[1]
    assert w % n_new == 0
    return pl.pallas_call(
        _set_tail_kernel,
        out_shape=jax.ShapeDtypeStruct(buf.shape, buf.dtype),
        grid=(s,),
        in_specs=[pl.BlockSpec((None, n_new) + rest, lambda b: (b, 0) + zeros),
                  pl.BlockSpec(memory_space=pl.ANY)],
        out_specs=pl.BlockSpec((None, n_new) + rest,
                               lambda b: (seq_base + b, w // n_new - 1) + zeros),
        input_output_aliases={1: 0},
        compiler_params=_cparams(("arbitrary",)),
        name="window_append",
    )(new_rows, buf)


def _combine_a_kernel(*refs, dils):
    n = len(dils)
    o_refs, l_refs, out_ref, stage_ref = refs[:n], refs[n:2 * n], refs[2 * n], refs[2 * n + 1]
    load = lambda ref, dil: ref[...] if dil == 1 else _lanes_to_rows(ref, dil, stage_ref)
    lses = [load(r, dil) for r, dil in zip(l_refs, dils)]
    m = functools.reduce(jnp.maximum, lses)
    ws = [jnp.exp(l - m) for l in lses]
    num = sum(w * load(r, dil) for w, r, dil in zip(ws, o_refs, dils))
    out_ref[...] = (num / sum(ws)).astype(out_ref.dtype)


def _combine_a(os_, lses, dils, tm):
    width = os_[0].shape[1] // dils[0]
    r = os_[0].shape[0] * dils[0]
    specs = [pl.BlockSpec((tm // dil, dil * width), lambda t: (t, 0)) for dil in dils]
    return pl.pallas_call(
        functools.partial(_combine_a_kernel, dils=tuple(dils)),
        out_shape=jax.ShapeDtypeStruct((r, width), BF16),
        grid=(r // tm,),
        in_specs=specs * 2,
        out_specs=pl.BlockSpec((tm, width), lambda t: (t, 0)),
        scratch_shapes=[pltpu.VMEM((width // LANES, tm, LANES), F32)],
        compiler_params=_cparams(("arbitrary",)),
        name="dilated_combine",
    )(*os_, *lses)


def _plan_a():
    plan = []
    for g in range(len(A_GROUPS)):
        base = g * 3 * A_WIDTH
        plan.append((base, A_WIDTH, A_WIDTH, QK_SCALE, ((g, 0),)))
        plan.append((base + A_WIDTH, A_WIDTH, A_WIDTH, 1.0, ((3 + g, 0), (6 + g, 0))))
        plan.append((base + 2 * A_WIDTH, A_WIDTH, 0, 1.0, ((3 + g, A_WIDTH), (6 + g, A_WIDTH))))
    return plan


def _mixer_a_prompt(x, mod3, tables, w_in, bsz, seq_len, tm):
    keeps = [min(win, seq_len) for win, _ in A_GROUPS]
    dils = [dil for _, dil in A_GROUPS]
    outs = ([(A_WIDTH, BF16, ("dil", dil)) for dil in dils]
            + [(2 * A_WIDTH, BF16, ("dil", dil)) for dil in dils]
            + [(2 * A_WIDTH, F32, k) for k in keeps])
    res = _inproj(x, mod3, tables, w_in, _plan_a(), outs, tm, seq_len)
    os_, lses = [], []
    for g, dil in enumerate(dils):
        ld = seq_len // dil
        o, lse = _attn_a_prompt(res[g].reshape(bsz, ld, dil * A_WIDTH),
                                res[3 + g].reshape(bsz, ld, 2 * dil * A_WIDTH), dil)
        os_.append(o.reshape(bsz * ld, dil * A_WIDTH))
        lses.append(lse.reshape(bsz * ld, dil * A_WIDTH))
    o = _combine_a(os_, lses, dils, tm)
    bufs = [res[6 + g].reshape(bsz, keeps[g], 2, A_HEADS, HEAD_DIM) for g in range(3)]
    return o, bufs


def _mixer_a_sample(x, mod3, tables, w_in, states, bufs, j, n_seq, n_new):
    outs = [(A_WIDTH, F32, None)] * 3 + [(2 * A_WIDTH, F32, None)] * 3 + [(2 * A_WIDTH, F32, None)] * 3
    res = _inproj(x, mod3, tables, w_in, _plan_a(), outs, x.shape[0], n_new)
    os_, lses, new_bufs = [], [], []
    for g, (window, dil) in enumerate(A_GROUPS):
        kvn = res[3 + g].reshape(n_seq, n_new, 2, A_HEADS, HEAD_DIM)
        wq = res[g].reshape(n_seq, n_new, A_HEADS, HEAD_DIM).transpose(0, 2, 1, 3).reshape(
            n_seq, A_HEADS * n_new, HEAD_DIM).astype(BF16)
        o, lse = _attn_a_sample(wq, states[g], j * n_seq, kvn, dil, window)
        o = o.reshape(n_seq, A_HEADS, n_new, HEAD_DIM).transpose(0, 2, 1, 3)
        lse = lse.reshape(n_seq, A_HEADS, n_new, 1).transpose(0, 2, 1, 3)
        os_.append(o.reshape(n_seq * n_new, A_WIDTH))
        lses.append(jnp.broadcast_to(lse, o.shape).reshape(n_seq * n_new, A_WIDTH))
        new_bufs.append(_set_tail(bufs[g], kvn, j * n_seq))
    o = _combine_a(os_, lses, [1] * len(A_GROUPS), x.shape[0])
    return o, new_bufs


def _lambda_of(lam_ref, lam_init):
    lp = lam_ref[...]
    a = jnp.sum(lp[0:1, :] * lp[1:2, :], axis=1, keepdims=True)
    b = jnp.sum(lp[2:3, :] * lp[3:4, :], axis=1, keepdims=True)
    return jnp.exp(a) - jnp.exp(b) + lam_init


def _sub_ln(of, g, lam_init):
    ms = jnp.mean(of * of, axis=-1, keepdims=True)
    return of * lax.rsqrt(ms + LN_EPS) * g * (1.0 - lam_init)


def _init_softmax_state(m_sc, l_sc, acc_sc):
    m_sc[...] = jnp.full(m_sc.shape, NEG_INF, F32)
    l_sc[...] = jnp.zeros(l_sc.shape, F32)
    acc_sc[...] = jnp.zeros(acc_sc.shape, F32)


def _softmax_step_t(s, v, m_sc, l_sc, acc_sc, may_be_empty=False, cols=slice(None)):
    m_old = m_sc[:, cols]
    m_new = jnp.maximum(m_old, jnp.max(s, axis=0, keepdims=True))
    m_use = jnp.where(m_new == NEG_INF, 0.0, m_new) if may_be_empty else m_new
    alpha = jnp.exp(m_old - m_use)
    p = jnp.exp(s - m_use)
    l_sc[:, cols] = alpha * l_sc[:, cols] + jnp.sum(p, axis=0, keepdims=True)
    acc_sc[:, cols] = alpha * acc_sc[:, cols] + _dot_tn(v, p.astype(BF16))
    m_sc[:, cols] = m_new


def _attn_b_prompt_kernel(qi_tab, ki_tab, q_ref, k_ref, v_ref, lam_ref, g_ref, o_ref,
                          m_sc, l_sc, acc_sc, *, lam_init):
    tq = q_ref.shape[0]
    t = pl.program_id(2)
    qi = qi_tab[t]
    ki = ki_tab[t]

    @pl.when(ki == 0)
    def _():
        _init_softmax_state(m_sc, l_sc, acc_sc)

    def step(diagonal):
        qs = _pair_stack(q_ref[...])
        s = _dot_nt(k_ref[...], qs)
        if diagonal:
            kpos = lax.broadcasted_iota(jnp.int32, s.shape, 0)
            qpos = lax.broadcasted_iota(jnp.int32, s.shape, 1) % tq
            s = jnp.where(kpos <= qpos, s, NEG_INF)
        _softmax_step_t(s, v_ref[...], m_sc, l_sc, acc_sc)

    @pl.when(ki < qi)
    def _():
        step(False)

    @pl.when(ki == qi)
    def _():
        step(True)
        lam = _lambda_of(lam_ref, lam_init)
        o = acc_sc[...] / l_sc[...]
        of = o[:, :tq] - lam * o[:, tq:]
        ms = jnp.mean(of * of, axis=0, keepdims=True)
        of = jnp.transpose(of * lax.rsqrt(ms + LN_EPS))
        o_ref[...] = (of * g_ref[...] * (1.0 - lam_init)).astype(o_ref.dtype)


def _attn_b_prompt(q, kv, lam_p, subln_g, lam_init, tq):
    b, l, hw = q.shape
    nh = hw // B_VDIM
    nq = l // tq
    pairs = [(i, k) for i in range(nq) for k in range(i + 1)]
    qi_tab = jnp.array([p[0] for p in pairs], jnp.int32)
    ki_tab = jnp.array([p[1] for p in pairs], jnp.int32)
    kernel = functools.partial(_attn_b_prompt_kernel, lam_init=lam_init)
    grid_spec = pltpu.PrefetchScalarGridSpec(
        num_scalar_prefetch=2,
        grid=(b, nh, len(pairs)),
        in_specs=[
            pl.BlockSpec((None, tq, B_VDIM), lambda b_, h, t, qt, kt: (b_, qt[t], h)),
            pl.BlockSpec((None, tq, B_VDIM), lambda b_, h, t, qt, kt: (b_, kt[t], h)),
            pl.BlockSpec((None, tq, B_VDIM), lambda b_, h, t, qt, kt: (b_, kt[t], nh + h)),
            pl.BlockSpec((4, HEAD_DIM), lambda *_: (0, 0)),
            pl.BlockSpec((1, B_VDIM), lambda *_: (0, 0)),
        ],
        out_specs=pl.BlockSpec((None, tq, B_VDIM), lambda b_, h, t, qt, kt: (b_, qt[t], h)),
        scratch_shapes=[pltpu.VMEM((1, 2 * tq), F32), pltpu.VMEM((1, 2 * tq), F32),
                        pltpu.VMEM((B_VDIM, 2 * tq), F32)],
    )
    return pl.pallas_call(
        kernel,
        out_shape=jax.ShapeDtypeStruct(q.shape, BF16),
        grid_spec=grid_spec,
        compiler_params=_cparams(("arbitrary",) * 3),
        name="diff_attn_prompt",
    )(qi_tab, ki_tab, q, kv, kv, lam_p, subln_g.reshape(1, B_VDIM))


def _page_specs(n_per_step, n_pages, page_base, block, tail=None):
    n_steps = n_pages // n_per_step
    tail = (0,) * (len(block) - 1) if tail is None else tail

    def spec(e):
        def index(b, p, pt):
            page = jnp.minimum(p, n_steps - 1) * n_per_step + e
            return (page_base + pt[b * n_pages + page],) + tail
        return pl.BlockSpec(block, index)

    return [spec(e) for e in range(n_per_step)]


def _pad_rows(a, rows):
    return jnp.concatenate([a, jnp.zeros((rows - a.shape[0], a.shape[1]), a.dtype)], axis=0)


B_PAGES_PER_STEP = 4


def _attn_b_sample_kernel(pt_ref, wq_ref, *refs, lam_init, n_new, n_per_step):
    k_refs, v_refs = refs[:n_per_step], refs[n_per_step:2 * n_per_step]
    kn_ref, vn_ref, lam_ref, g_ref, o_ref, m_sc, l_sc, acc_sc = refs[2 * n_per_step:]
    nh = k_refs[0].shape[1]
    cols = wq_ref.shape[0]
    p_id = pl.program_id(1)
    n_steps = pl.num_programs(1) - 1
    same_head = (lax.broadcasted_iota(jnp.int32, (nh, cols), 0)
                 == lax.broadcasted_iota(jnp.int32, (nh, cols), 1) // (2 * n_new))

    def flat(ref_list):
        return jnp.concatenate([r[...].reshape(-1, B_VDIM) for r in ref_list], axis=0).astype(BF16)

    def scores(k, admit):
        s = _dot_nt(k, wq_ref[...])
        s3 = s.reshape(s.shape[0] // nh, nh, cols)
        return jnp.where(admit, s3, NEG_INF).reshape(s.shape)

    @pl.when(p_id == 0)
    def _():
        _init_softmax_state(m_sc, l_sc, acc_sc)

    @pl.when(p_id < n_steps)
    def _():
        _softmax_step_t(scores(flat(k_refs), same_head[None]), flat(v_refs), m_sc, l_sc, acc_sc)

    @pl.when(p_id == n_steps)
    def _():
        jk = lax.broadcasted_iota(jnp.int32, (n_new, nh, cols), 0)
        jq = lax.broadcasted_iota(jnp.int32, (n_new, nh, cols), 2) % n_new
        admit = jnp.logical_and(same_head[None], jk <= jq)
        _softmax_step_t(scores(flat([kn_ref]), admit), flat([vn_ref]), m_sc, l_sc, acc_sc)
        lam = _lambda_of(lam_ref, lam_init)
        o = jnp.transpose(acc_sc[...] / l_sc[...])
        for h in range(nh):
            r0 = h * 2 * n_new
            of = o[r0:r0 + n_new, :] - lam * o[r0 + n_new:r0 + 2 * n_new, :]
            o_ref[:, h * B_VDIM:(h + 1) * B_VDIM] = _sub_ln(of, g_ref[...], lam_init).astype(
                o_ref.dtype)


def _attn_b_sample(wq, pool, page_base, kvn, page_table, lam_p, subln_g, lam_init):
    s, cols, _ = wq.shape
    n_new, nh = kvn.shape[1], kvn.shape[3]
    n_pages = page_table.shape[1]
    pps = B_PAGES_PER_STEP
    assert n_pages % pps == 0 and cols == LANES
    kernel = functools.partial(_attn_b_sample_kernel, lam_init=lam_init, n_new=n_new,
                               n_per_step=pps)
    page_blk = (None, PAGE_SIZE, None, nh, B_VDIM)
    new_blk = (None, n_new, None, nh, B_VDIM)
    grid_spec = pltpu.PrefetchScalarGridSpec(
        num_scalar_prefetch=1,
        grid=(s, n_pages // pps + 1),
        in_specs=[pl.BlockSpec((None, cols, B_VDIM), lambda b, p, pt: (b, 0, 0))]
        + _page_specs(pps, n_pages, page_base, page_blk, (0, 0, 0, 0))
        + _page_specs(pps, n_pages, page_base, page_blk, (0, 1, 0, 0))
        + [
            pl.BlockSpec(new_blk, lambda b, p, pt: (b, 0, 0, 0, 0)),
            pl.BlockSpec(new_blk, lambda b, p, pt: (b, 0, 1, 0, 0)),
            pl.BlockSpec((4, HEAD_DIM), lambda *_: (0, 0)),
            pl.BlockSpec((1, B_VDIM), lambda *_: (0, 0)),
        ],
        out_specs=pl.BlockSpec((None, n_new, nh * B_VDIM), lambda b, p, pt: (b, 0, 0)),
        scratch_shapes=[pltpu.VMEM((1, cols), F32), pltpu.VMEM((1, cols), F32),
                        pltpu.VMEM((B_VDIM, cols), F32)],
    )
    return pl.pallas_call(
        kernel,
        out_shape=jax.ShapeDtypeStruct((s, n_new, nh * B_VDIM), BF16),
        grid_spec=grid_spec,
        compiler_params=_cparams(("arbitrary", "arbitrary")),
        name="diff_attn_sample",
    )(page_table.reshape(-1), wq, *([pool] * (2 * pps)), kvn, kvn, lam_p,
      subln_g.reshape(1, B_VDIM))


def _plan_b():
    hw = B_HEADS * B_VDIM
    half = hw // 2
    return [
        (0, half, half, QK_SCALE, ((0, 0),)),
        (half, half, half, QK_SCALE, ((0, half),)),
        (hw, half, half, 1.0, ((1, 0), (2, 0))),
        (hw + half, half, half, 1.0, ((1, half), (2, half))),
        (2 * hw, half, 0, 1.0, ((1, hw), (2, hw))),
        (2 * hw + half, half, 0, 1.0, ((1, hw + half), (2, hw + half))),
    ]


def _mixer_b_prompt(x, mod3, tables, w_in, lam_p, subln_g, lam_init, bsz, seq_len, tm):
    hw = B_HEADS * B_VDIM
    outs = [(hw, BF16, None), (2 * hw, BF16, None), (2 * hw, F32, None)]
    q, kvb, kvf = _inproj(x, mod3, tables, w_in, _plan_b(), outs, tm, seq_len)
    o = _attn_b_prompt(q.reshape(bsz, seq_len, hw), kvb.reshape(bsz, seq_len, 2 * hw),
                       lam_p, subln_g, lam_init, min(512, seq_len))
    kv = kvf.reshape(bsz * seq_len // PAGE_SIZE, PAGE_SIZE, 2, B_HEADS, B_VDIM)
    return o.reshape(bsz * seq_len, hw), kv


def _mixer_b_sample(x, mod3, tables, w_in, lam_p, subln_g, lam_init, pools, j, page_table,
                    n_seq, n_new):
    hw = B_HEADS * B_VDIM
    outs = [(hw, BF16, None), (2 * hw, F32, None), (2 * hw, F32, None)]
    q, _, kvf = _inproj(x, mod3, tables, w_in, _plan_b(), outs, x.shape[0], n_new)
    q4 = q.reshape(n_seq, n_new, B_HEADS, 2, HEAD_DIM)
    eye_2 = jnp.eye(2, dtype=BF16)
    wq = (q4.transpose(0, 2, 3, 1, 4)[:, :, :, :, None, :]
          * eye_2[None, None, :, None, :, None]).reshape(n_seq, B_HEADS * 2 * n_new, B_VDIM)
    kvn = kvf.reshape(n_seq, n_new, 2, B_HEADS, B_VDIM)
    n_phys = pools.shape[1]
    pool = pools.reshape((pools.shape[0] * n_phys,) + pools.shape[2:])
    o = _attn_b_sample(wq, pool, j * n_phys, kvn, page_table, lam_p, subln_g, lam_init)
    return o.reshape(n_seq * n_new, hw), kvn


def _sortable_key(score):
    bits = lax.bitcast_convert_type(score + 0.0, jnp.int32)
    return bits ^ ((bits >> 31) & jnp.int32(0x7FFFFFFF))


def _kth_largest_key(count_ge, shape, k):
    t0 = jnp.full(shape, INT_MIN, jnp.int32)
    t0 = jnp.where(count_ge(jnp.zeros(shape, jnp.int32)) >= k, jnp.zeros(shape, jnp.int32), t0)

    def body(it, t):
        cand = t + jnp.left_shift(jnp.int32(1), 30 - it)
        return jnp.where(count_ge(cand) >= k, cand, t)

    return lax.fori_loop(0, 31, body, t0)


def _select_prompt_kernel(qi_ref, kiw_all_ref, kiw_q_ref, sel_ref, sc_ref, *, topk):
    tq = qi_ref.shape[0]
    l = kiw_all_ref.shape[0]
    i = pl.program_id(1)
    tk = min(1024, l)
    n_pairs = qi_ref.shape[1] // LANES
    wi = kiw_q_ref[:, IDX_DIM:IDX_DIM + IDX_HEADS] * IDX_W_SCALE
    qs = jnp.concatenate([_pair_stack(qi_ref[:, p * LANES:(p + 1) * LANES])
                          for p in range(n_pairs)], axis=0)
    for c in range(l // tk):
        kic = kiw_all_ref[c * tk:(c + 1) * tk, :]
        lane = lax.broadcasted_iota(jnp.int32, kic.shape, 1)
        ki2 = jnp.where(lane < IDX_DIM, kic, pltpu.roll(kic, IDX_DIM, 1)).astype(BF16)
        r = jnp.maximum(_dot_nt(qs, ki2), 0.0)
        sc = jnp.zeros((tq, tk), F32)
        for h in range(IDX_HEADS):
            sc = sc + r[h * tq:(h + 1) * tq, :] * wi[:, h:h + 1]
        row = lax.broadcasted_iota(jnp.int32, sc.shape, 0) + i * tq
        col = lax.broadcasted_iota(jnp.int32, sc.shape, 1) + c * tk
        sc_ref[:, c * tk:(c + 1) * tk] = _sortable_key(jnp.where(col <= row, sc, NEG_INF))

    def count_ge(t):
        return jnp.sum((sc_ref[...] >= t).astype(jnp.int32), axis=1, keepdims=True)

    thr = _kth_largest_key(count_ge, (tq, 1), topk)
    n_gt = jnp.sum((sc_ref[...] > thr).astype(jnp.int32), axis=1, keepdims=True)
    need = (topk - n_gt).astype(F32)
    tri = (lax.broadcasted_iota(jnp.int32, (LANES, LANES), 0)
           < lax.broadcasted_iota(jnp.int32, (LANES, LANES), 1)).astype(BF16)
    offs = jnp.zeros((tq, 1), F32)
    row = lax.broadcasted_iota(jnp.int32, (tq, LANES), 0) + i * tq
    lane = lax.broadcasted_iota(jnp.int32, (tq, LANES), 1)
    for c in range(l // LANES):
        keys = sc_ref[:, c * LANES:(c + 1) * LANES]
        causal = lane + c * LANES <= row
        eq = jnp.logical_and(keys == thr, causal)
        e = jnp.where(eq, 1.0, 0.0)
        rank = _dot(e.astype(BF16), tri) + offs
        sel = jnp.logical_or(keys > thr, jnp.logical_and(eq, rank < need))
        sel_ref[:, c * LANES:(c + 1) * LANES] = jnp.where(sel, 1.0, 0.0).astype(sel_ref.dtype)
        offs = offs + jnp.sum(e, axis=1, keepdims=True)


def _select_prompt(qi, kiw, topk):
    b, l, w = qi.shape
    tq = PAGE_SIZE
    return pl.pallas_call(
        functools.partial(_select_prompt_kernel, topk=topk),
        out_shape=jax.ShapeDtypeStruct((b, l, l), BF16),
        grid=(b, l // tq),
        in_specs=[
            pl.BlockSpec((None, tq, w), lambda b_, i: (b_, i, 0)),
            pl.BlockSpec((None, l, LANES), lambda b_, i: (b_, 0, 0)),
            pl.BlockSpec((None, tq, LANES), lambda b_, i: (b_, i, 0)),
        ],
        out_specs=pl.BlockSpec((None, tq, l), lambda b_, i: (b_, i, 0)),
        scratch_shapes=[pltpu.VMEM((tq, l), jnp.int32)],
        compiler_params=_cparams(("arbitrary", "arbitrary")),
        name="indexer_select_prompt",
    )(qi, kiw, kiw)


def _attn_c_prompt_kernel(q_ref, k_ref, v_ref, sel_ref, o_ref, m_sc, l_sc, acc_sc):
    tq = q_ref.shape[0]
    n_q_chunks = q_ref.shape[1] // LANES
    per_kv = n_q_chunks // (k_ref.shape[1] // LANES)
    qi = pl.program_id(1)
    ki = pl.program_id(2)
    tk = k_ref.shape[0]
    last = (qi * tq + tq - 1) // tk

    @pl.when(ki == 0)
    def _():
        m_sc[...] = jnp.full(m_sc.shape, NEG_INF, F32)
        l_sc[...] = jnp.zeros(l_sc.shape, F32)
        acc_sc[...] = jnp.zeros(acc_sc.shape, F32)

    @pl.when(ki <= last)
    def _():
        sel = sel_ref[...].astype(F32)
        rows_kv = 2 * per_kv * tq
        selx = jnp.concatenate([sel] * (2 * per_kv), axis=0) > 0.5
        for a in range(k_ref.shape[1] // LANES):
            qs = jnp.concatenate(
                [_pair_stack(q_ref[:, (a * per_kv + e) * LANES:(a * per_kv + e + 1) * LANES])
                 for e in range(per_kv)], axis=0)
            s = _dot_nt(qs, k_ref[:, a * LANES:(a + 1) * LANES])
            s = jnp.where(selx, s, NEG_INF)
            rs = slice(a * rows_kv, (a + 1) * rows_kv)
            m_old = m_sc[rs, :]
            m_new = jnp.maximum(m_old, jnp.max(s, axis=1, keepdims=True))
            m_use = jnp.where(m_new == NEG_INF, 0.0, m_new)
            alpha = jnp.exp(m_old - m_use)
            p = jnp.exp(s - m_use)
            l_sc[rs, :] = alpha * l_sc[rs, :] + jnp.sum(p, axis=1, keepdims=True)
            acc_sc[rs, :] = alpha * acc_sc[rs, :] + _dot(p.astype(BF16),
                                                         v_ref[:, a * LANES:(a + 1) * LANES])
            m_sc[rs, :] = m_new

    @pl.when(ki == pl.num_programs(2) - 1)
    def _():
        o = acc_sc[...] / l_sc[...]
        for c in range(n_q_chunks):
            top = o[(2 * c) * tq:(2 * c + 1) * tq, :]
            bot = o[(2 * c + 1) * tq:(2 * c + 2) * tq, :]
            o_ref[:, c * LANES:(c + 1) * LANES] = _pair_merge(top, bot).astype(o_ref.dtype)


def _attn_c_prompt(q, kd, vd, sel, tk):
    b, l, qw = q.shape
    tq = PAGE_SIZE
    kw = kd.shape[2]
    nk = l // tk
    rows = 2 * (qw // LANES) * tq
    last = lambda i: (i * tq + tq - 1) // tk
    return pl.pallas_call(
        _attn_c_prompt_kernel,
        out_shape=jax.ShapeDtypeStruct(q.shape, BF16),
        grid=(b, l // tq, nk),
        in_specs=[
            pl.BlockSpec((None, tq, qw), lambda b_, i, k: (b_, i, 0)),
            pl.BlockSpec((None, tk, kw), lambda b_, i, k: (b_, jnp.minimum(k, last(i)), 0)),
            pl.BlockSpec((None, tk, kw), lambda b_, i, k: (b_, jnp.minimum(k, last(i)), 0)),
            pl.BlockSpec((None, tq, tk), lambda b_, i, k: (b_, i, jnp.minimum(k, last(i)))),
        ],
        out_specs=pl.BlockSpec((None, tq, qw), lambda b_, i, k: (b_, i, 0)),
        scratch_shapes=[pltpu.VMEM((rows, 1), F32), pltpu.VMEM((rows, 1), F32),
                        pltpu.VMEM((rows, LANES), F32)],
        compiler_params=_cparams(("arbitrary",) * 3),
        name="sparse_attn_prompt",
    )(q, kd, vd, sel)


KEY_NEG_INF = -2139095041
SELECT_ROW_CHUNK = 512


def _select_prompt_t_kernel(qi_ref, kiw_all_ref, kiw_q_ref, sel_ref, key_sc, *, topk):
    tq = qi_ref.shape[0]
    l = kiw_all_ref.shape[0]
    i = pl.program_id(1)
    rc = min(SELECT_ROW_CHUNK, l)
    n_pairs = qi_ref.shape[1] // LANES
    q_hi = i * tq + tq - 1
    wi_t = jnp.transpose(kiw_q_ref[...])[IDX_DIM:IDX_DIM + IDX_HEADS, :] * IDX_W_SCALE
    qs = jnp.concatenate([_pair_stack(qi_ref[:, p * LANES:(p + 1) * LANES])
                          for p in range(n_pairs)], axis=0)
    qpos = lax.broadcasted_iota(jnp.int32, (rc, tq), 1) + i * tq
    krow = lax.broadcasted_iota(jnp.int32, (rc, tq), 0)
    for c in range(l // rc):
        rows = slice(c * rc, (c + 1) * rc)

        @pl.when(c * rc <= q_hi)
        def _(c=c, rows=rows):
            kic = kiw_all_ref[rows, :]
            lane = lax.broadcasted_iota(jnp.int32, kic.shape, 1)
            ki2 = jnp.where(lane < IDX_DIM, kic, pltpu.roll(kic, IDX_DIM, 1)).astype(BF16)
            r = jnp.maximum(_dot_nt(ki2, qs), 0.0)
            sc = jnp.zeros((rc, tq), F32)
            for h in range(IDX_HEADS):
                sc = sc + r[:, h * tq:(h + 1) * tq] * wi_t[h:h + 1, :]
            key_sc[rows, :] = _sortable_key(jnp.where(krow + c * rc <= qpos, sc, NEG_INF))

        @pl.when(c * rc > q_hi)
        def _(rows=rows):
            key_sc[rows, :] = jnp.full((rc, tq), KEY_NEG_INF, jnp.int32)

    n_scan = q_hi // rc + 1

    def count(pred):
        def body(c, acc):
            k = key_sc[pl.ds(pl.multiple_of(c * rc, rc), rc), :]
            return acc + jnp.sum(jnp.where(pred(k), 1, 0).reshape(rc // 8, 8, tq), axis=0)
        acc = lax.fori_loop(0, n_scan, body, jnp.zeros((8, tq), jnp.int32))
        return jnp.sum(acc, axis=0, keepdims=True)

    def count_ge(t):
        return count(lambda k: k >= t) + jnp.where(t <= KEY_NEG_INF, l - n_scan * rc, 0)

    thr = _kth_largest_key(count_ge, (1, tq), topk)
    need = (topk - count(lambda k: k > thr)).astype(F32)
    tri = (lax.broadcasted_iota(jnp.int32, (LANES, LANES), 1)
           < lax.broadcasted_iota(jnp.int32, (LANES, LANES), 0)).astype(BF16)
    qpos = lax.broadcasted_iota(jnp.int32, (LANES, tq), 1) + i * tq
    krow = lax.broadcasted_iota(jnp.int32, (LANES, tq), 0)
    offs = jnp.zeros((1, tq), F32)
    for c in range(l // LANES):
        keys = key_sc[c * LANES:(c + 1) * LANES, :]
        eq = jnp.logical_and(keys == thr, krow + c * LANES <= qpos)
        e = jnp.where(eq, 1.0, 0.0)
        rank = _dot(tri, e.astype(BF16)) + offs
        sel = jnp.logical_or(keys > thr, jnp.logical_and(eq, rank < need))
        sel_ref[c * LANES:(c + 1) * LANES, :] = jnp.where(sel, 1.0, 0.0).astype(sel_ref.dtype)
        offs = offs + jnp.sum(e, axis=0, keepdims=True)


def _select_prompt_t(qi, kiw, topk):
    b, l, w = qi.shape
    tq = PAGE_SIZE
    return pl.pallas_call(
        functools.partial(_select_prompt_t_kernel, topk=topk),
        out_shape=jax.ShapeDtypeStruct((b, l, l), BF16),
        grid=(b, l // tq),
        in_specs=[
            pl.BlockSpec((None, tq, w), lambda b_, i: (b_, i, 0)),
            pl.BlockSpec((None, l, LANES), lambda b_, i: (b_, 0, 0)),
            pl.BlockSpec((None, tq, LANES), lambda b_, i: (b_, i, 0)),
        ],
        out_specs=pl.BlockSpec((None, l, tq), lambda b_, i: (b_, 0, i)),
        scratch_shapes=[pltpu.VMEM((l, tq), jnp.int32)],
        compiler_params=_cparams(("arbitrary", "arbitrary")),
        name="indexer_select_prompt",
    )(qi, kiw, kiw)


def _attn_c_prompt_t_kernel(qi_tab, ki_tab, q_ref, k_ref, v_ref, sel_ref, o_ref,
                            m_sc, l_sc, acc_sc):
    tq = q_ref.shape[0]
    tk = k_ref.shape[0]
    n_kv = k_ref.shape[1] // LANES
    per_kv = q_ref.shape[1] // LANES // n_kv
    cols_kv = 2 * per_kv * tq
    t = pl.program_id(1)
    qi = qi_tab[t]
    ki = ki_tab[t]

    @pl.when(ki == 0)
    def _():
        _init_softmax_state(m_sc, l_sc, acc_sc)

    sel = sel_ref[...].astype(F32)
    selx = jnp.concatenate([sel] * (2 * per_kv), axis=1) > 0.5
    for a in range(n_kv):
        qs = jnp.concatenate(
            [_pair_stack(q_ref[:, (a * per_kv + e) * LANES:(a * per_kv + e + 1) * LANES])
             for e in range(per_kv)], axis=0)
        s = jnp.where(selx, _dot_nt(k_ref[:, a * LANES:(a + 1) * LANES], qs), NEG_INF)
        _softmax_step_t(s, v_ref[:, a * LANES:(a + 1) * LANES], m_sc, l_sc, acc_sc,
                        may_be_empty=True, cols=slice(a * cols_kv, (a + 1) * cols_kv))

    @pl.when(ki == (qi * tq + tq - 1) // tk)
    def _():
        o = acc_sc[...] / l_sc[...]
        for c in range(n_kv * per_kv):
            lo = o[0:HEAD_DIM, (2 * c) * tq:(2 * c + 1) * tq]
            hi = o[0:HEAD_DIM, (2 * c + 1) * tq:(2 * c + 2) * tq]
            o_ref[:, c * LANES:(c + 1) * LANES] = jnp.transpose(
                jnp.concatenate([lo, hi], axis=0)).astype(o_ref.dtype)


def _attn_c_prompt_t(q, kd, vd, sel_t, tk):
    b, l, qw = q.shape
    tq = PAGE_SIZE
    kw = kd.shape[2]
    cols = 2 * (qw // LANES) * tq
    pairs = [(i, k) for i in range(l // tq) for k in range((i * tq + tq - 1) // tk + 1)]
    qi_tab = jnp.array([p[0] for p in pairs], jnp.int32)
    ki_tab = jnp.array([p[1] for p in pairs], jnp.int32)
    grid_spec = pltpu.PrefetchScalarGridSpec(
        num_scalar_prefetch=2,
        grid=(b, len(pairs)),
        in_specs=[
            pl.BlockSpec((None, tq, qw), lambda b_, t, qt, kt: (b_, qt[t], 0)),
            pl.BlockSpec((None, tk, kw), lambda b_, t, qt, kt: (b_, kt[t], 0)),
            pl.BlockSpec((None, tk, kw), lambda b_, t, qt, kt: (b_, kt[t], 0)),
            pl.BlockSpec((None, tk, tq), lambda b_, t, qt, kt: (b_, kt[t], qt[t])),
        ],
        out_specs=pl.BlockSpec((None, tq, qw), lambda b_, t, qt, kt: (b_, qt[t], 0)),
        scratch_shapes=[pltpu.VMEM((1, cols), F32), pltpu.VMEM((1, cols), F32),
                        pltpu.VMEM((LANES, cols), F32)],
    )
    return pl.pallas_call(
        _attn_c_prompt_t_kernel,
        out_shape=jax.ShapeDtypeStruct(q.shape, BF16),
        grid_spec=grid_spec,
        compiler_params=_cparams(("arbitrary", "arbitrary")),
        name="sparse_attn_prompt",
    )(qi_tab, ki_tab, q, kd, vd, sel_t)


C_PAGES_PER_STEP = 8
C_SELECT_PAGES_PER_STEP = 16


def _select_sample_kernel(pt_ref, qs_ref, wi_ref, *refs, topk, n_new, n_per_step):
    page_refs = refs[:n_per_step]
    kin_ref, sel_ref, key_sc = refs[n_per_step:]
    p_id = pl.program_id(1)
    n_steps = pl.num_programs(1) - 1
    n_pages = key_sc.shape[0] - 1
    wi = wi_ref[...] * IDX_W_SCALE

    def scores(ki_t):
        r = jnp.maximum(_dot(qs_ref[...], ki_t), 0.0)
        sc = jnp.zeros((n_new, PAGE_SIZE), F32)
        for h in range(IDX_HEADS):
            sc = sc + r[h * n_new:(h + 1) * n_new, :] * wi[:, h:h + 1]
        return sc

    @pl.when(p_id < n_steps)
    def _():
        for e, page_ref in enumerate(page_refs):
            key_sc[p_id * n_per_step + e] = _sortable_key(scores(page_ref[...].astype(BF16)))

    @pl.when(p_id == n_steps)
    def _():
        for e in range(1, sel_ref.shape[0] - n_pages):
            sel_ref[n_pages + e] = jnp.zeros((n_new, PAGE_SIZE), F32)
        sc = scores(jnp.transpose(_pad_rows(kin_ref[...], PAGE_SIZE)).astype(BF16))
        jq = lax.broadcasted_iota(jnp.int32, sc.shape, 0)
        jk = lax.broadcasted_iota(jnp.int32, sc.shape, 1)
        key_sc[n_pages] = _sortable_key(jnp.where(jk <= jq, sc, NEG_INF))

        def count(pred):
            c = jnp.where(pred(key_sc[...]), 1, 0)
            return jnp.sum(jnp.sum(c, axis=0), axis=1, keepdims=True)

        thr = _kth_largest_key(lambda t: count(lambda k: k >= t[None]), (n_new, 1), topk)
        need = (topk - count(lambda k: k > thr[None])).astype(F32)
        tri = (lax.broadcasted_iota(jnp.int32, (LANES, LANES), 0)
               < lax.broadcasted_iota(jnp.int32, (LANES, LANES), 1)).astype(BF16)

        def body(p, offs):
            keys = key_sc[p]
            valid = jnp.logical_or(p < n_pages, jk <= jq)
            eq = jnp.logical_and(keys == thr, valid)
            e = jnp.where(eq, 1.0, 0.0)
            rank = _dot(e.astype(BF16), tri) + offs
            sel = jnp.logical_or(keys > thr, jnp.logical_and(eq, rank < need))
            sel_ref[p] = jnp.where(sel, 1.0, 0.0)
            return offs + jnp.sum(e, axis=1, keepdims=True)

        lax.fori_loop(0, n_pages + 1, body, jnp.zeros((n_new, 1), F32))


def _select_sample(qs, wi, pool_idx, page_base, ki_new, page_table, topk):
    s, rows, _ = qs.shape
    n_new = ki_new.shape[1]
    n_pages = page_table.shape[1]
    pps = min(C_SELECT_PAGES_PER_STEP, n_pages)
    n_sel = n_pages + C_PAGES_PER_STEP
    assert n_pages % pps == 0
    kernel = functools.partial(_select_sample_kernel, topk=topk, n_new=n_new, n_per_step=pps)
    grid_spec = pltpu.PrefetchScalarGridSpec(
        num_scalar_prefetch=1,
        grid=(s, n_pages // pps + 1),
        in_specs=[
            pl.BlockSpec((None, rows, IDX_DIM), lambda b, p, pt: (b, 0, 0)),
            pl.BlockSpec((None, n_new, IDX_HEADS), lambda b, p, pt: (b, 0, 0)),
        ] + _page_specs(pps, n_pages, page_base, (None, IDX_DIM, PAGE_SIZE)) + [
            pl.BlockSpec((None, n_new, IDX_DIM), lambda b, p, pt: (b, 0, 0)),
        ],
        out_specs=pl.BlockSpec((None, n_sel, n_new, PAGE_SIZE), lambda b, p, pt: (b, 0, 0, 0)),
        scratch_shapes=[pltpu.VMEM((n_pages + 1, n_new, PAGE_SIZE), jnp.int32)],
    )
    return pl.pallas_call(
        kernel,
        out_shape=jax.ShapeDtypeStruct((s, n_sel, n_new, PAGE_SIZE), F32),
        grid_spec=grid_spec,
        compiler_params=_cparams(("arbitrary", "arbitrary")),
        name="indexer_select_sample",
    )(page_table.reshape(-1), qs, wi, *([pool_idx] * pps), ki_new)


def _attn_c_sample_kernel(pt_ref, wq_ref, *refs, n_new, n_per_step):
    page_refs = refs[:n_per_step]
    kvn_ref, sel_ref, o_ref, m_sc, l_sc, acc_sc = refs[n_per_step:]
    kw = wq_ref.shape[1]
    n_rep = wq_ref.shape[0] // n_new
    p_id = pl.program_id(1)
    n_steps = pl.num_programs(1) - 1

    @pl.when(p_id == 0)
    def _():
        _init_softmax_state(m_sc, l_sc, acc_sc)

    def update(k_t, v_t, sel):
        sel = jnp.concatenate([sel] * n_rep, axis=0) > 0.5
        s = jnp.where(sel, _dot(wq_ref[...], k_t), NEG_INF)
        m_old = m_sc[...]
        m_new = jnp.maximum(m_old, jnp.max(s, axis=1, keepdims=True))
        m_use = jnp.where(m_new == NEG_INF, 0.0, m_new)
        alpha = jnp.exp(m_old - m_use)
        p = jnp.exp(s - m_use)
        l_sc[...] = alpha * l_sc[...] + jnp.sum(p, axis=1, keepdims=True)
        acc_sc[...] = alpha * acc_sc[...] + _dot_nt(p.astype(BF16), v_t)
        m_sc[...] = m_new

    @pl.when(p_id < n_steps)
    def _():
        update(jnp.concatenate([r[0].astype(BF16) for r in page_refs], axis=1),
               jnp.concatenate([r[1].astype(BF16) for r in page_refs], axis=1),
               jnp.concatenate([sel_ref[e] for e in range(n_per_step)], axis=1))

    @pl.when(p_id == n_steps)
    def _():
        new_t = lambda a: jnp.transpose(_pad_rows(a, PAGE_SIZE)).astype(BF16)
        update(new_t(kvn_ref[:, 0:kw]), new_t(kvn_ref[:, kw:2 * kw]), sel_ref[0])
        o_ref[...] = acc_sc[...] / l_sc[...]


def _attn_c_sample(wq, pool, page_base, kvn, sel, page_table):
    s, rows, kw = wq.shape
    n_new = kvn.shape[1]
    n_pages = page_table.shape[1]
    pps = C_PAGES_PER_STEP
    grid_spec = pltpu.PrefetchScalarGridSpec(
        num_scalar_prefetch=1,
        grid=(s, n_pages // pps + 1),
        in_specs=[pl.BlockSpec((None, rows, kw), lambda b, p, pt: (b, 0, 0))]
        + _page_specs(pps, n_pages, page_base, (None, 2, kw, PAGE_SIZE))
        + [
            pl.BlockSpec((None, n_new, 2 * kw), lambda b, p, pt: (b, 0, 0)),
            pl.BlockSpec((None, pps, n_new, PAGE_SIZE), lambda b, p, pt: (b, p, 0, 0)),
        ],
        out_specs=pl.BlockSpec((None, rows, kw), lambda b, p, pt: (b, 0, 0)),
        scratch_shapes=[pltpu.VMEM((rows, 1), F32), pltpu.VMEM((rows, 1), F32),
                        pltpu.VMEM((rows, kw), F32)],
    )
    return pl.pallas_call(
        functools.partial(_attn_c_sample_kernel, n_new=n_new, n_per_step=pps),
        out_shape=jax.ShapeDtypeStruct((s, rows, kw), F32),
        grid_spec=grid_spec,
        compiler_params=_cparams(("arbitrary", "arbitrary")),
        name="sparse_attn_sample",
    )(page_table.reshape(-1), wq, *([pool] * pps), kvn, sel)


C_QW = C_HEADS * HEAD_DIM
C_KW = C_KV_HEADS * HEAD_DIM
C_IW = IDX_HEADS * IDX_DIM


def _prep_c_weights(w):
    q, k, v, qi, ki, wi = jnp.split(w, [C_QW, C_QW + C_KW, C_QW + 2 * C_KW,
                                        C_QW + 2 * C_KW + C_IW, C_QW + 2 * C_KW + C_IW + IDX_DIM],
                                    axis=1)
    dup = lambda a: jnp.repeat(a.reshape(a.shape[0], C_KV_HEADS, 1, HEAD_DIM), 2, axis=2).reshape(
        a.shape[0], 2 * C_KW)
    pad = jnp.zeros((w.shape[0], LANES - IDX_DIM - IDX_HEADS), w.dtype)
    return jnp.concatenate([q, k, v, qi, ki, wi, pad, dup(k), dup(v)], axis=1)


def _plan_c():
    o_kv = C_QW
    o_qi = C_QW + 2 * C_KW
    o_ki = o_qi + C_IW
    o_kd = o_ki + LANES
    o_vd = o_kd + 2 * C_KW
    return [
        (0, 512, 512, QK_SCALE, ((0, 0),)),
        (512, 512, 512, QK_SCALE, ((0, 512),)),
        (o_kv, 2 * C_KW, C_KW, 1.0, ((1, 0),)),
        (o_qi, C_IW, C_IW, 1.0, ((2, 0),)),
        (o_ki, LANES, IDX_DIM, 1.0, ((3, 0),)),
        (o_kd, 2 * C_KW, 2 * C_KW, 1.0, ((4, 0),)),
        (o_vd, 2 * C_KW, 0, 1.0, ((5, 0),)),
    ]


def _mixer_c_prompt(x, mod3, tables, w_prep, bsz, seq_len, tm):
    outs = [(C_QW, BF16, None), (2 * C_KW, F32, None), (C_IW, BF16, None), (LANES, F32, None),
            (2 * C_KW, BF16, None), (2 * C_KW, BF16, None)]
    q, kvf, qi, kiw, kd, vd = _inproj(x, mod3, tables, w_prep, _plan_c(), outs, tm, seq_len)
    topk = min(C_TOPK_MAX, seq_len // 4)
    r3 = lambda a: a.reshape(bsz, seq_len, a.shape[-1])
    sel_t = _select_prompt_t(r3(qi), r3(kiw), topk)
    o = _attn_c_prompt_t(r3(q), r3(kd), r3(vd), sel_t, min(512, seq_len))
    n_pg = bsz * seq_len // PAGE_SIZE
    kv = kvf.reshape(n_pg, PAGE_SIZE, 2, C_KV_HEADS, HEAD_DIM)
    ki = kiw[:, :IDX_DIM].reshape(n_pg, PAGE_SIZE, IDX_DIM)
    return o.reshape(bsz * seq_len, C_QW), kv, ki


def _mixer_c_sample(x, mod3, tables, w_prep, pools_kv, pools_idx, j, page_table, n_seq, n_new):
    n_phys = pools_kv.shape[1]
    pool_kv = jnp.transpose(pools_kv, (0, 1, 3, 4, 5, 2)).reshape(
        pools_kv.shape[0] * n_phys, 2, C_KW, PAGE_SIZE)
    pool_idx = jnp.transpose(pools_idx, (0, 1, 3, 2)).reshape(
        pools_idx.shape[0] * n_phys, IDX_DIM, PAGE_SIZE)
    outs = [(C_QW, BF16, None), (2 * C_KW, F32, None), (C_IW, BF16, None), (LANES, F32, None),
            (2 * C_KW, BF16, None), (2 * C_KW, BF16, None)]
    q, kvf, qi, kiw, _, _ = _inproj(x, mod3, tables, w_prep, _plan_c(), outs, x.shape[0], n_new)
    n_past = page_table.shape[1] * PAGE_SIZE
    topk = min(C_TOPK_MAX, (n_past + n_new) // 4)
    qs = qi.reshape(n_seq, n_new, IDX_HEADS, IDX_DIM).transpose(0, 2, 1, 3).reshape(
        n_seq, IDX_HEADS * n_new, IDX_DIM)
    kiw3 = kiw.reshape(n_seq, n_new, LANES)
    ki_new = kiw3[:, :, :IDX_DIM]
    wi = kiw3[:, :, IDX_DIM:IDX_DIM + IDX_HEADS]
    sel = _select_sample(qs, wi, pool_idx, j * n_phys, ki_new, page_table, topk)
    q4 = q.reshape(n_seq, n_new, C_HEADS, HEAD_DIM).transpose(0, 2, 1, 3)
    kv_of = jnp.arange(C_HEADS) // (C_HEADS // C_KV_HEADS)
    onehot = (kv_of[:, None] == jnp.arange(C_KV_HEADS)[None, :]).astype(BF16)
    wq = (q4[:, :, :, None, :] * onehot[None, :, None, :, None]).reshape(
        n_seq, C_HEADS * n_new, C_KW)
    kvn = kvf.reshape(n_seq, n_new, 2 * C_KW)
    acc = _attn_c_sample(wq, pool_kv, j * n_phys, kvn, sel, page_table)
    acc = acc.reshape(n_seq, C_HEADS, n_new, C_KV_HEADS, HEAD_DIM)
    o = jnp.einsum("shjad,ha->sjhd", acc, onehot.astype(F32))
    o = o.reshape(n_seq * n_new, C_QW).astype(BF16)
    return (o, kvf.reshape(n_seq, n_new, 2, C_KV_HEADS, HEAD_DIM),
            ki_new.reshape(n_seq, n_new, IDX_DIM))


def kernel(x_prompt, x_sample, state_a_kv_w128, state_a_kv_w512, state_a_kv_w2048, cache_b_kv,
           cache_c_kv, cache_c_idx, page_table, c_prompt, c_sample, w_mod, b_mod, ln_g, ln_b,
           w_ffn_in, w_ffn_out, a_w_in, a_w_out, b_w_in, b_w_out, b_lambda, b_subln_g, c_w_in,
           c_w_out):
    bsz, seq_len, d = x_prompt.shape
    n_seq, n_new, _ = x_sample.shape
    n_past = page_table.shape[1] * PAGE_SIZE
    a_states = (state_a_kv_w128, state_a_kv_w512, state_a_kv_w2048)
    tm = min(512, seq_len)
    rs = n_seq * n_new

    n_c = bsz + n_seq
    n_c_pad = -(-n_c // 8) * 8
    c_all = jnp.concatenate([c_prompt, c_sample, jnp.zeros((n_c_pad - n_c, d), F32)], axis=0)
    mod_all = _modulation(c_all, w_mod, b_mod)

    tables_p = _rope_tables(jnp.arange(seq_len, dtype=jnp.int32))
    tables_s = tuple(jnp.tile(t, (n_seq, 1)) for t in
                     _rope_tables(n_past + jnp.arange(n_new, dtype=jnp.int32)))

    xp = x_prompt.reshape(bsz * seq_len, d)
    xs = x_sample.reshape(rs, d)
    new_a_p = [[] for _ in A_GROUPS]
    a_flat = [s.reshape((s.shape[0] * n_seq,) + s.shape[2:]) for s in a_states]
    a_bufs = [_shift_state(s, n_new) for s in a_flat]
    b_p, b_s, ckv_p, ckv_s, cidx_p, cidx_s = [], [], [], [], [], []

    for i in range(DEPTH):
        kind, j = i % N_MIXERS, i // N_MIXERS
        mod_p = mod_all[i, :bsz][:, None, :]
        mod_s = jnp.repeat(mod_all[i, bsz:bsz + n_seq], n_new, axis=0)[None]
        if kind == 0:
            w_in = a_w_in[j].astype(BF16)
            w_out = a_w_out[j].astype(BF16)
            op, bufs_p = _mixer_a_prompt(xp, mod_p, tables_p, w_in, bsz, seq_len, tm)
            os_, a_bufs = _mixer_a_sample(xs, mod_s, tables_s, w_in, a_flat, a_bufs, j,
                                          n_seq, n_new)
            for g in range(len(A_GROUPS)):
                new_a_p[g].append(bufs_p[g])
        elif kind == 1:
            w_in = b_w_in[j].astype(BF16)
            w_out = b_w_out[j].astype(BF16)
            lam_init = 0.8 - 0.6 * math.exp(-0.3 * i)
            op, kv_p = _mixer_b_prompt(xp, mod_p, tables_p, w_in, b_lambda[j], b_subln_g[j],
                                       lam_init, bsz, seq_len, tm)
            os_, kv_s = _mixer_b_sample(xs, mod_s, tables_s, w_in, b_lambda[j], b_subln_g[j],
                                        lam_init, cache_b_kv, j, page_table, n_seq, n_new)
            b_p.append(kv_p)
            b_s.append(kv_s)
        else:
            w_in = _prep_c_weights(c_w_in[j]).astype(BF16)
            w_out = c_w_out[j].astype(BF16)
            op, kv_p, ki_p = _mixer_c_prompt(xp, mod_p, tables_p, w_in, bsz, seq_len, tm)
            os_, kv_s, ki_s = _mixer_c_sample(xs, mod_s, tables_s, w_in, cache_c_kv, cache_c_idx,
                                              j, page_table, n_seq, n_new)
            ckv_p.append(kv_p)
            ckv_s.append(kv_s)
            cidx_p.append(ki_p)
            cidx_s.append(ki_s)
        w1 = w_ffn_in[i].astype(BF16)
        w2 = w_ffn_out[i].astype(BF16)
        xp = _post(xp, op, mod_p, w_out, ln_g[i], ln_b[i], w1, w2, tm, seq_len)
        xs = _post(xs, os_, mod_s, w_out, ln_g[i], ln_b[i], w1, w2, rs, n_new)

    y_prompt = xp.reshape(bsz, seq_len, d)
    y_sample = xs.reshape(n_seq, n_new, d)
    stack = lambda xs: xs[0][None] if len(xs) == 1 else jnp.stack(xs)
    a_p = [stack(a) for a in new_a_p]
    a_s = [b.reshape(s.shape) for b, s in zip(a_bufs, a_states)]
    return (y_prompt, y_sample, a_p[0], a_s[0], a_p[1], a_s[1], a_p[2], a_s[2],
            stack(b_p), stack(b_s), stack(ckv_p), stack(ckv_s), stack(cidx_p), stack(cidx_s))
```

```python
import functools
import math

import jax
import jax.numpy as jnp
from jax import lax
from jax.experimental import pallas as pl
from jax.experimental.pallas import tpu as pltpu

F32 = jnp.float32
BF16 = jnp.bfloat16

DEPTH = 4
N_MIXERS = 3
D_MODEL = 1024
HEAD_DIM = 64
PAGE_SIZE = 128
ROPE_THETA = 500000.0
LN_EPS = 1e-5
DEEPNORM_ALPHA = (2 * DEPTH) ** 0.25
QK_SCALE = HEAD_DIM ** -0.5

A_GROUPS = ((128, 1), (512, 4), (2048, 16))
A_HEADS = 8
A_WIDTH = A_HEADS * HEAD_DIM
B_HEADS = 8
B_VDIM = 2 * HEAD_DIM
C_HEADS = 16
C_KV_HEADS = 4
IDX_HEADS = 8
IDX_DIM = 64
C_TOPK_MAX = 256
IDX_W_SCALE = IDX_HEADS ** -0.5 * IDX_DIM ** -0.5
FFN_HIDDEN = 2816

LANES = 128
VMEM_LIMIT_BYTES = 56 * 1024 * 1024
NEG_INF = float("-inf")
INT_MIN = -(2 ** 31)


def _cparams(semantics, vmem=VMEM_LIMIT_BYTES):
    return pltpu.CompilerParams(dimension_semantics=semantics, vmem_limit_bytes=vmem)


def _dot(a, b):
    return jnp.dot(a, b, preferred_element_type=F32)


def _dot_nt(a, b):
    return lax.dot_general(a, b, (((1,), (1,)), ((), ())), preferred_element_type=F32)


def _dot_tn(a, b):
    return lax.dot_general(a, b, (((0,), (0,)), ((), ())), preferred_element_type=F32)


def _resident(shape):
    nd = len(shape)
    return pl.BlockSpec(shape, lambda *_: (0,) * nd, pipeline_mode=pl.Buffered(1))


def _split_bf16(a):
    hi = a.astype(BF16)
    lo = (a - hi.astype(F32)).astype(BF16)
    return hi, lo


def _mod_kernel(c_ref, w_ref, b_ref, o_ref):
    c = c_ref[...]
    a_hi, a_lo = _split_bf16(jax.nn.silu(c))
    w_hi, w_lo = _split_bf16(w_ref[...])
    o_ref[...] = _dot(a_hi, w_hi) + _dot(a_hi, w_lo) + _dot(a_lo, w_hi) + b_ref[...]


def _modulation(c_all, w_mod, b_mod):
    n, d = c_all.shape
    depth, _, n6 = w_mod.shape
    tn = 1536
    return pl.pallas_call(
        _mod_kernel,
        out_shape=jax.ShapeDtypeStruct((depth, n, n6), F32),
        grid=(depth, n6 // tn),
        in_specs=[
            pl.BlockSpec((n, d), lambda i, j: (0, 0)),
            pl.BlockSpec((None, d, tn), lambda i, j: (i, 0, j)),
            pl.BlockSpec((None, 1, tn), lambda i, j: (i, 0, j)),
        ],
        out_specs=pl.BlockSpec((None, n, tn), lambda i, j: (i, 0, j)),
        compiler_params=_cparams(("arbitrary", "arbitrary")),
        name="adaln_modulation",
    )(c_all, w_mod, b_mod.reshape(depth, 1, n6))


def _rope_tables(pos):
    rot = HEAD_DIM // 4
    half = rot // 2
    inv = ROPE_THETA ** (-jnp.arange(0, rot, 2, dtype=F32) / rot)
    ang = pos.astype(F32)[:, None] * inv[None, :]
    cos, sin = jnp.cos(ang), jnp.sin(ang)
    n = pos.shape[0]
    ones = jnp.ones((n, HEAD_DIM - rot), F32)
    zeros = jnp.zeros((n, HEAD_DIM - rot), F32)
    zh = jnp.zeros((n, half), F32)
    c = jnp.concatenate([cos, cos, ones], axis=1)
    sa = jnp.concatenate([-sin, zh, zeros], axis=1)
    sb = jnp.concatenate([zh, sin, zeros], axis=1)
    rep = LANES // HEAD_DIM
    return jnp.tile(c, (1, rep)), jnp.tile(sa, (1, rep)), jnp.tile(sb, (1, rep))


INPROJ_SEG = 512


def _rows_to_lanes(y, o_ref, col, dil, stage_ref):
    tm, wd = y.shape
    slot = o_ref.shape[1] // dil
    for c in range(wd // LANES):
        stage_ref[c] = y[:, c * LANES:(c + 1) * LANES]
    for r in range(dil):
        for c in range(wd // LANES):
            lo = r * slot + col + c * LANES
            o_ref[:, lo:lo + LANES] = stage_ref[c, pl.ds(r, tm // dil, stride=dil), :].astype(
                o_ref.dtype)


def _lanes_to_rows(x_ref, dil, stage_ref):
    n, tot = x_ref.shape
    wd = tot // dil
    for r in range(dil):
        for c in range(wd // LANES):
            lo = r * wd + c * LANES
            stage_ref[c, pl.ds(r, n, stride=dil), :] = x_ref[:, lo:lo + LANES].astype(F32)
    return jnp.concatenate([stage_ref[c] for c in range(wd // LANES)], axis=1)


def _inproj_kernel(x_ref, mod_ref, cos_ref, sa_ref, sb_ref, w_ref, *refs,
                   plan, keeps, tiles_per_seq):
    out_refs, stage_ref = refs[:-1], refs[-1]
    d = x_ref.shape[1]
    tm = x_ref.shape[0]
    x = x_ref[...]
    h = (x * (1.0 + mod_ref[:, d:2 * d]) + mod_ref[:, 0:d]).astype(BF16)
    cos, sa, sb = cos_ref[...], sa_ref[...], sb_ref[...]
    half = HEAD_DIM // 8
    t_in = pl.program_id(0) % tiles_per_seq
    for c0, wd, rope_cols, scale, dests in plan:
        y = _dot(h, w_ref[:, c0:c0 + wd])
        if rope_cols:
            reps = wd // LANES
            tile = lambda t: jnp.concatenate([t] * reps, axis=1) if reps > 1 else t
            yr = (y * tile(cos) + pltpu.roll(y, wd - half, 1) * tile(sa)
                  + pltpu.roll(y, half, 1) * tile(sb))
            if rope_cols < wd:
                lane = lax.broadcasted_iota(jnp.int32, y.shape, 1)
                y = jnp.where(lane < rope_cols, yr, y)
            else:
                y = yr
        if scale != 1.0:
            y = y * scale
        for oi, col in dests:
            o_ref = out_refs[oi]
            keep = keeps[oi]
            if keep is None:
                o_ref[:, col:col + wd] = y.astype(o_ref.dtype)
            elif isinstance(keep, tuple):
                if keep[1] == 1:
                    o_ref[:, col:col + wd] = y.astype(o_ref.dtype)
                else:
                    _rows_to_lanes(y, o_ref, col, keep[1], stage_ref)
            elif keep >= tm:
                first = tiles_per_seq - keep // tm

                @pl.when(t_in >= first)
                def _(o_ref=o_ref, y=y, col=col, wd=wd):
                    o_ref[:, col:col + wd] = y.astype(o_ref.dtype)
            else:
                @pl.when(t_in == tiles_per_seq - 1)
                def _(o_ref=o_ref, y=y, col=col, wd=wd, keep=keep):
                    o_ref[:, col:col + wd] = y[tm - keep:, :].astype(o_ref.dtype)


def _inproj(x, mod3, tables, w, plan, outs, tm, seq_len):
    r, d = x.shape
    n_tiles = r // tm
    tps = seq_len // tm if mod3.shape[1] == 1 else 1
    tbl_tiles = tables[0].shape[0] // tm
    rm = mod3.shape[1]
    out_shapes, out_specs, keeps = [], [], []
    for cols, dtype, keep in outs:
        keeps.append(keep)
        if keep is None:
            out_shapes.append(jax.ShapeDtypeStruct((r, cols), dtype))
            out_specs.append(pl.BlockSpec((tm, cols), lambda t: (t, 0)))
        elif isinstance(keep, tuple):
            dil = keep[1]
            assert tm % (16 * dil) == 0
            out_shapes.append(jax.ShapeDtypeStruct((r // dil, dil * cols), dtype))
            out_specs.append(pl.BlockSpec((tm // dil, dil * cols), lambda t: (t, 0)))
        elif keep >= tm:
            assert keep % tm == 0 and mod3.shape[1] == 1
            first = tps - keep // tm
            out_shapes.append(jax.ShapeDtypeStruct((r // seq_len, keep, cols), dtype))
            out_specs.append(pl.BlockSpec(
                (None, tm, cols),
                lambda t, first=first: (t // tps, jnp.maximum(t % tps - first, 0), 0)))
        else:
            assert tm % keep == 0 and keep % 8 == 0 and mod3.shape[1] == 1
            out_shapes.append(jax.ShapeDtypeStruct((r // seq_len, keep, cols), dtype))
            out_specs.append(pl.BlockSpec((None, keep, cols), lambda t: (t // tps, 0, 0)))
    tbl_spec = pl.BlockSpec((tm, LANES), lambda t: (t % tbl_tiles, 0))
    kernel = functools.partial(_inproj_kernel, plan=tuple(plan), keeps=tuple(keeps),
                               tiles_per_seq=tps)
    return pl.pallas_call(
        kernel,
        out_shape=out_shapes,
        grid=(n_tiles,),
        in_specs=[
            pl.BlockSpec((tm, d), lambda t: (t, 0)),
            pl.BlockSpec((None, rm, 2 * d), lambda t: (t // tps, 0, 0)),
            tbl_spec, tbl_spec, tbl_spec,
            _resident(w.shape),
        ],
        out_specs=out_specs,
        scratch_shapes=[pltpu.VMEM((INPROJ_SEG // LANES, tm, LANES), F32)],
        compiler_params=_cparams(("arbitrary",)),
        name="mixer_in_proj",
    )(x, mod3, *tables, w)


def _layer_norm(x, g, b):
    mu = jnp.mean(x, axis=-1, keepdims=True)
    xc = x - mu
    var = jnp.mean(xc * xc, axis=-1, keepdims=True)
    return xc * lax.rsqrt(var + LN_EPS) * g + b


FFN_CHUNK = 256


def _post_kernel(x_ref, o_ref, mod_ref, wo_ref, lng_ref, lnb_ref, w1_ref, w2_ref,
                 y_ref, acc_ref):
    d = x_ref.shape[1]
    f = w2_ref.shape[0]
    x = x_ref[...]
    g1 = mod_ref[:, 2 * d:3 * d]
    sh2 = mod_ref[:, 3 * d:4 * d]
    sc2 = mod_ref[:, 4 * d:5 * d]
    g2 = mod_ref[:, 5 * d:6 * d]
    m = _dot(o_ref[...], wo_ref[...])
    x1 = _layer_norm(DEEPNORM_ALPHA * x + (1.0 + g1) * m, lng_ref[0:1, :], lnb_ref[0:1, :])
    h = (x1 * (1.0 + sc2) + sh2).astype(BF16)
    for c in range(f // FFN_CHUNK):
        c0 = c * FFN_CHUNK
        g = _dot(h, w1_ref[:, c0:c0 + FFN_CHUNK])
        u = _dot(h, w1_ref[:, f + c0:f + c0 + FFN_CHUNK])
        part = _dot((jax.nn.silu(g) * u).astype(BF16), w2_ref[c0:c0 + FFN_CHUNK, :])
        if c == 0:
            acc_ref[...] = part
        else:
            acc_ref[...] += part
    y_ref[...] = _layer_norm(DEEPNORM_ALPHA * x1 + (1.0 + g2) * acc_ref[...],
                             lng_ref[1:2, :], lnb_ref[1:2, :])


def _post(x, o, mod3, wo, lng, lnb, w1, w2, tm, seq_len):
    r, d = x.shape
    wo_in = o.shape[1]
    rm = mod3.shape[1]
    tps = seq_len // tm if rm == 1 else 1
    return pl.pallas_call(
        _post_kernel,
        out_shape=jax.ShapeDtypeStruct((r, d), F32),
        grid=(r // tm,),
        in_specs=[
            pl.BlockSpec((tm, d), lambda t: (t, 0)),
            pl.BlockSpec((tm, wo_in), lambda t: (t, 0)),
            pl.BlockSpec((None, rm, 6 * d), lambda t: (t // tps, 0, 0)),
            _resident(wo.shape), _resident(lng.shape), _resident(lnb.shape),
            _resident(w1.shape), _resident(w2.shape),
        ],
        out_specs=pl.BlockSpec((tm, d), lambda t: (t, 0)),
        scratch_shapes=[pltpu.VMEM((tm, d), F32)],
        compiler_params=_cparams(("arbitrary",)),
        name="out_proj_ffn",
    )(x, o, mod3, wo, lng, lnb, w1, w2)


def _pair_stack(qp):
    lane = lax.broadcasted_iota(jnp.int32, qp.shape, 1)
    lo = lane < HEAD_DIM
    zero = jnp.zeros_like(qp)
    return jnp.concatenate([jnp.where(lo, qp, zero), jnp.where(lo, zero, qp)], axis=0)


def _pair_merge(top, bot):
    lane = lax.broadcasted_iota(jnp.int32, top.shape, 1)
    return jnp.where(lane < HEAD_DIM, top, bot)


def _attn_a_prompt_kernel(q_ref, kvc_ref, kvp_ref, o_ref, lse_ref):
    tq = q_ref.shape[0]
    width = q_ref.shape[1]
    i = pl.program_id(2)
    krow = lax.broadcasted_iota(jnp.int32, (tq, 2 * tq), 0)
    qcol = lax.broadcasted_iota(jnp.int32, (tq, 2 * tq), 1) % tq
    mask_c = krow <= qcol
    mask_p = jnp.logical_and(krow >= qcol, i > 0)
    for j in range(width // LANES):
        sl = slice(j * LANES, (j + 1) * LANES)
        vsl = slice(width + j * LANES, width + (j + 1) * LANES)
        qs = _pair_stack(q_ref[:, sl])
        s_c = jnp.where(mask_c, _dot_nt(kvc_ref[:, sl], qs), NEG_INF)
        s_p = jnp.where(mask_p, _dot_nt(kvp_ref[:, sl], qs), NEG_INF)
        m = jnp.maximum(jnp.max(s_c, axis=0, keepdims=True), jnp.max(s_p, axis=0, keepdims=True))
        p_c = jnp.exp(s_c - m)
        p_p = jnp.exp(s_p - m)
        l = jnp.sum(p_c, axis=0, keepdims=True) + jnp.sum(p_p, axis=0, keepdims=True)
        ov = (_dot_tn(kvc_ref[:, vsl], p_c.astype(BF16))
              + _dot_tn(kvp_ref[:, vsl], p_p.astype(BF16))) / l
        lse = jnp.broadcast_to(m + jnp.log(l), (HEAD_DIM, 2 * tq))
        o_ref[:, sl] = jnp.transpose(
            jnp.concatenate([ov[:HEAD_DIM, :tq], ov[HEAD_DIM:, tq:]], axis=0))
        lse_ref[:, sl] = jnp.transpose(jnp.concatenate([lse[:, :tq], lse[:, tq:]], axis=0))


def _attn_a_prompt(q, kv, dil):
    b, ld, tot = q.shape
    width = tot // dil
    tq = PAGE_SIZE
    nb = ld // tq
    o, lse = pl.pallas_call(
        _attn_a_prompt_kernel,
        out_shape=[jax.ShapeDtypeStruct(q.shape, F32), jax.ShapeDtypeStruct(q.shape, F32)],
        grid=(b, dil, nb),
        in_specs=[
            pl.BlockSpec((None, tq, width), lambda b_, r, i: (b_, i, r)),
            pl.BlockSpec((None, tq, 2 * width), lambda b_, r, i: (b_, i, r)),
            pl.BlockSpec((None, tq, 2 * width), lambda b_, r, i: (b_, jnp.maximum(i - 1, 0), r)),
        ],
        out_specs=[pl.BlockSpec((None, tq, width), lambda b_, r, i: (b_, i, r))] * 2,
        compiler_params=_cparams(("arbitrary",) * 3),
        name="dilated_attn_prompt",
    )(q, kv, kv)
    return o, lse


A_SAMPLE_KEY_CHUNK = 512


def _attn_a_sample_kernel(wq_ref, k_ref, v_ref, kn_ref, vn_ref, o_ref, lse_ref,
                          m_sc, l_sc, acc_sc, *, dil, window, w_buf):
    kc, nh, _ = k_ref.shape
    cols = wq_ref.shape[0]
    n_new = cols // nh
    c = pl.program_id(1)

    def flat(ref):
        return ref[...].reshape(-1, HEAD_DIM).astype(BF16)

    def step(k_ref_, v_ref_, key0):
        nk = k_ref_.shape[0]
        s = _dot_nt(flat(k_ref_), wq_ref[...]).reshape(nk, nh, cols)
        key = lax.broadcasted_iota(jnp.int32, s.shape, 0) + key0
        head = lax.broadcasted_iota(jnp.int32, s.shape, 1)
        col = lax.broadcasted_iota(jnp.int32, s.shape, 2)
        off = w_buf + col % n_new - key
        admit = jnp.logical_and(head == col // n_new,
                                jnp.logical_and(jnp.logical_and(off >= 0, off <= window),
                                                (off & (dil - 1)) == 0))
        s = jnp.where(admit, s, NEG_INF).reshape(nk * nh, cols)
        _softmax_step_t(s, flat(v_ref_), m_sc, l_sc, acc_sc, may_be_empty=True)

    @pl.when(c == 0)
    def _():
        _init_softmax_state(m_sc, l_sc, acc_sc)

    step(k_ref, v_ref, c * kc)

    @pl.when(c == pl.num_programs(1) - 1)
    def _():
        step(kn_ref, vn_ref, w_buf)
        l = l_sc[...]
        o_ref[...] = jnp.transpose(acc_sc[...] / l)
        lse_ref[...] = m_sc[...] + jnp.log(l)


def _attn_a_sample(wq, state, seq_base, kvn, dil, window):
    s, cols, _ = wq.shape
    w_buf, nh = state.shape[1], state.shape[3]
    n_new = kvn.shape[1]
    kc = min(A_SAMPLE_KEY_CHUNK, w_buf)
    assert w_buf % kc == 0
    blk = lambda rows: (None, rows, None, nh, HEAD_DIM)
    kernel = functools.partial(_attn_a_sample_kernel, dil=dil, window=window, w_buf=w_buf)
    return pl.pallas_call(
        kernel,
        out_shape=[jax.ShapeDtypeStruct((s, cols, HEAD_DIM), F32),
                   jax.ShapeDtypeStruct((s, 1, cols), F32)],
        grid=(s, w_buf // kc),
        in_specs=[
            pl.BlockSpec((None, cols, HEAD_DIM), lambda b, c: (b, 0, 0)),
            pl.BlockSpec(blk(kc), lambda b, c: (seq_base + b, c, 0, 0, 0)),
            pl.BlockSpec(blk(kc), lambda b, c: (seq_base + b, c, 1, 0, 0)),
            pl.BlockSpec(blk(n_new), lambda b, c: (b, 0, 0, 0, 0)),
            pl.BlockSpec(blk(n_new), lambda b, c: (b, 0, 1, 0, 0)),
        ],
        out_specs=[pl.BlockSpec((None, cols, HEAD_DIM), lambda b, c: (b, 0, 0)),
                   pl.BlockSpec((None, 1, cols), lambda b, c: (b, 0, 0))],
        scratch_shapes=[pltpu.VMEM((1, cols), F32), pltpu.VMEM((1, cols), F32),
                        pltpu.VMEM((HEAD_DIM, cols), F32)],
        compiler_params=_cparams(("arbitrary", "arbitrary")),
        name="dilated_attn_sample",
    )(wq, state, state, kvn, kvn)


SHIFT_ROW_CHUNK = 512


def _shift_state_kernel(cur_ref, nxt_ref, o_ref):
    rc, n_new = cur_ref.shape[0], nxt_ref.shape[0]
    c = pl.program_id(1)
    last = pl.num_programs(1) - 1
    o_ref[0:rc - n_new] = cur_ref[n_new:rc]

    @pl.when(c < last)
    def _():
        o_ref[rc - n_new:rc] = nxt_ref[...]

    @pl.when(c == last)
    def _():
        o_ref[rc - n_new:rc] = jnp.zeros(nxt_ref.shape, o_ref.dtype)


def _shift_state(state, n_new):
    n, w = state.shape[:2]
    rest = state.shape[2:]
    zeros = (0,) * len(rest)
    rc = min(SHIFT_ROW_CHUNK, w)
    assert w % rc == 0 and rc % n_new == 0
    per = rc // n_new
    return pl.pallas_call(
        _shift_state_kernel,
        out_shape=jax.ShapeDtypeStruct(state.shape, state.dtype),
        grid=(n, w // rc),
        in_specs=[
            pl.BlockSpec((None, rc) + rest, lambda b, c: (b, c) + zeros),
            pl.BlockSpec((None, n_new) + rest,
                         lambda b, c: (b, jnp.minimum((c + 1) * per, w // n_new - 1)) + zeros),
        ],
        out_specs=pl.BlockSpec((None, rc) + rest, lambda b, c: (b, c) + zeros),
        compiler_params=_cparams(("arbitrary", "arbitrary")),
        name="window_shift",
    )(state, state)


def _set_tail_kernel(new_ref, buf_ref, o_ref):
    del buf_ref
    o_ref[...] = new_ref[...]


def _set_tail(buf, new_rows, seq_base):
    s, n_new = new_rows.shape[:2]
    rest = new_rows.shape[2:]
    zeros = (0,) * len(rest)
    w = buf.shape[1]
    assert w % n_new == 0
    return pl.pallas_call(
        _set_tail_kernel,
        out_shape=jax.ShapeDtypeStruct(buf.shape, buf.dtype),
        grid=(s,),
        in_specs=[pl.BlockSpec((None, n_new) + rest, lambda b: (b, 0) + zeros),
                  pl.BlockSpec(memory_space=pl.ANY)],
        out_specs=pl.BlockSpec((None, n_new) + rest,
                               lambda b: (seq_base + b, w // n_new - 1) + zeros),
        input_output_aliases={1: 0},
        compiler_params=_cparams(("arbitrary",)),
        name="window_append",
    )(new_rows, buf)


def _attn_a_sample_t_kernel(qbd_ref, kv_ref, kvn_ref, o_ref, lse_ref, m_sc, l_sc, acc_sc,
                            *, dil, window, w_buf):
    rows, width = qbd_ref.shape
    kc = kv_ref.shape[2]
    n_new = kvn_ref.shape[0]
    nh = rows // n_new
    c = pl.program_id(1)

    def step(k_t, v_t, key0):
        s = _dot(qbd_ref[...], k_t)
        jq = lax.broadcasted_iota(jnp.int32, s.shape, 0) % n_new
        key = lax.broadcasted_iota(jnp.int32, s.shape, 1) + key0
        off = w_buf + jq - key
        admit = jnp.logical_and(jnp.logical_and(off >= 0, off <= window),
                                (off & (dil - 1)) == 0)
        s = jnp.where(admit, s, NEG_INF)
        m_old = m_sc[...]
        m_new = jnp.maximum(m_old, jnp.max(s, axis=1, keepdims=True))
        m_use = jnp.where(m_new == NEG_INF, 0.0, m_new)
        alpha = jnp.exp(m_old - m_use)
        p = jnp.exp(s - m_use)
        l_sc[...] = alpha * l_sc[...] + jnp.sum(p, axis=1, keepdims=True)
        acc_sc[...] = alpha * acc_sc[...] + _dot_nt(p.astype(BF16), v_t)
        m_sc[...] = m_new

    @pl.when(c == 0)
    def _():
        _init_softmax_state(m_sc, l_sc, acc_sc)

    step(kv_ref[0].astype(BF16), kv_ref[1].astype(BF16), c * kc)

    @pl.when(c == pl.num_programs(1) - 1)
    def _():
        new_t = lambda a: jnp.transpose(_pad_rows(a, LANES)).astype(BF16)
        step(new_t(kvn_ref[:, 0:width]), new_t(kvn_ref[:, width:2 * width]), w_buf)
        l = l_sc[...]
        o = acc_sc[...] / l
        lse = m_sc[...] + jnp.log(l)
        for h in range(nh):
            rs = slice(h * n_new, (h + 1) * n_new)
            cs = slice(h * HEAD_DIM, (h + 1) * HEAD_DIM)
            o_ref[:, cs] = o[rs, cs]
            lse_ref[:, cs] = jnp.broadcast_to(lse[rs, :], (n_new, HEAD_DIM))


def _attn_a_sample_t(qbd, state_t, seq_base, kvn, dil, window):
    s, rows, width = qbd.shape
    w_buf = state_t.shape[3]
    n_new = kvn.shape[1]
    kc = min(A_SAMPLE_KEY_CHUNK, w_buf)
    assert w_buf % kc == 0
    kernel = functools.partial(_attn_a_sample_t_kernel, dil=dil, window=window, w_buf=w_buf)
    out_spec = pl.BlockSpec((None, n_new, width), lambda b, c: (b, 0, 0))
    return pl.pallas_call(
        kernel,
        out_shape=[jax.ShapeDtypeStruct((s, n_new, width), F32)] * 2,
        grid=(s, w_buf // kc),
        in_specs=[
            pl.BlockSpec((None, rows, width), lambda b, c: (b, 0, 0)),
            pl.BlockSpec((None, 2, width, kc), lambda b, c: (seq_base + b, 0, 0, c)),
            pl.BlockSpec((None, n_new, 2 * width), lambda b, c: (b, 0, 0)),
        ],
        out_specs=[out_spec, out_spec],
        scratch_shapes=[pltpu.VMEM((rows, 1), F32), pltpu.VMEM((rows, 1), F32),
                        pltpu.VMEM((rows, width), F32)],
        compiler_params=_cparams(("arbitrary", "arbitrary")),
        name="dilated_attn_sample",
    )(qbd, state_t, kvn)


def _shift_state_t_kernel(x_ref, o_ref, *, n_new):
    w = x_ref.shape[1]
    o_ref[...] = pltpu.roll(x_ref[...], w - n_new, 1)


def _shift_state_t(state_t, n_new):
    n, parts, width, w = state_t.shape
    spec = pl.BlockSpec((None, None, width, w), lambda b, t: (b, t, 0, 0))
    return pl.pallas_call(
        functools.partial(_shift_state_t_kernel, n_new=n_new),
        out_shape=jax.ShapeDtypeStruct(state_t.shape, state_t.dtype),
        grid=(n, parts),
        in_specs=[spec],
        out_specs=spec,
        compiler_params=_cparams(("arbitrary", "arbitrary")),
        name="window_shift",
    )(state_t)


def _set_tail_t_kernel(new_ref, buf_ref, o_ref):
    n_new = new_ref.shape[0]
    width = buf_ref.shape[1]
    lane = lax.broadcasted_iota(jnp.int32, (width, LANES), 1)
    for t in range(buf_ref.shape[0]):
        rows = jnp.concatenate([jnp.zeros((LANES - n_new, width), F32),
                                new_ref[:, t * width:(t + 1) * width]], axis=0)
        o_ref[t] = jnp.where(lane >= LANES - n_new, jnp.transpose(rows), buf_ref[t])


def _set_tail_t(buf_t, new_rows, seq_base):
    s, n_new, _ = new_rows.shape
    _, parts, width, w = buf_t.shape
    blk = pl.BlockSpec((None, parts, width, LANES), lambda b: (seq_base + b, 0, 0, w // LANES - 1))
    return pl.pallas_call(
        _set_tail_t_kernel,
        out_shape=jax.ShapeDtypeStruct(buf_t.shape, buf_t.dtype),
        grid=(s,),
        in_specs=[pl.BlockSpec((None, n_new, parts * width), lambda b: (b, 0, 0)), blk],
        out_specs=blk,
        input_output_aliases={1: 0},
        compiler_params=_cparams(("arbitrary",)),
        name="window_append",
    )(new_rows, buf_t)


def _combine_a_kernel(*refs, dils):
    n = len(dils)
    o_refs, l_refs, out_ref, stage_ref = refs[:n], refs[n:2 * n], refs[2 * n], refs[2 * n + 1]
    load = lambda ref, dil: ref[...] if dil == 1 else _lanes_to_rows(ref, dil, stage_ref)
    lses = [load(r, dil) for r, dil in zip(l_refs, dils)]
    m = functools.reduce(jnp.maximum, lses)
    ws = [jnp.exp(l - m) for l in lses]
    num = sum(w * load(r, dil) for w, r, dil in zip(ws, o_refs, dils))
    out_ref[...] = (num / sum(ws)).astype(out_ref.dtype)


def _combine_a(os_, lses, dils, tm):
    width = os_[0].shape[1] // dils[0]
    r = os_[0].shape[0] * dils[0]
    specs = [pl.BlockSpec((tm // dil, dil * width), lambda t: (t, 0)) for dil in dils]
    return pl.pallas_call(
        functools.partial(_combine_a_kernel, dils=tuple(dils)),
        out_shape=jax.ShapeDtypeStruct((r, width), BF16),
        grid=(r // tm,),
        in_specs=specs * 2,
        out_specs=pl.BlockSpec((tm, width), lambda t: (t, 0)),
        scratch_shapes=[pltpu.VMEM((width // LANES, tm, LANES), F32)],
        compiler_params=_cparams(("arbitrary",)),
        name="dilated_combine",
    )(*os_, *lses)


def _plan_a():
    plan = []
    for g in range(len(A_GROUPS)):
        base = g * 3 * A_WIDTH
        plan.append((base, A_WIDTH, A_WIDTH, QK_SCALE, ((g, 0),)))
        plan.append((base + A_WIDTH, A_WIDTH, A_WIDTH, 1.0, ((3 + g, 0), (6 + g, 0))))
        plan.append((base + 2 * A_WIDTH, A_WIDTH, 0, 1.0, ((3 + g, A_WIDTH), (6 + g, A_WIDTH))))
    return plan


def _mixer_a_prompt(x, mod3, tables, w_in, bsz, seq_len, tm):
    keeps = [min(win, seq_len) for win, _ in A_GROUPS]
    dils = [dil for _, dil in A_GROUPS]
    outs = ([(A_WIDTH, BF16, ("dil", dil)) for dil in dils]
            + [(2 * A_WIDTH, BF16, ("dil", dil)) for dil in dils]
            + [(2 * A_WIDTH, F32, k) for k in keeps])
    res = _inproj(x, mod3, tables, w_in, _plan_a(), outs, tm, seq_len)
    os_, lses = [], []
    for g, dil in enumerate(dils):
        ld = seq_len // dil
        o, lse = _attn_a_prompt(res[g].reshape(bsz, ld, dil * A_WIDTH),
                                res[3 + g].reshape(bsz, ld, 2 * dil * A_WIDTH), dil)
        os_.append(o.reshape(bsz * ld, dil * A_WIDTH))
        lses.append(lse.reshape(bsz * ld, dil * A_WIDTH))
    o = _combine_a(os_, lses, dils, tm)
    bufs = [res[6 + g].reshape(bsz, keeps[g], 2, A_HEADS, HEAD_DIM) for g in range(3)]
    return o, bufs


def _mixer_a_sample(x, mod3, tables, w_in, states, bufs, j, n_seq, n_new):
    outs = [(A_WIDTH, F32, None)] * 3 + [(2 * A_WIDTH, F32, None)] * 3 + [(2 * A_WIDTH, F32, None)] * 3
    res = _inproj(x, mod3, tables, w_in, _plan_a(), outs, x.shape[0], n_new)
    os_, lses, new_bufs = [], [], []
    eye = jnp.eye(A_HEADS, dtype=F32)
    for g, (window, dil) in enumerate(A_GROUPS):
        kvn = res[3 + g].reshape(n_seq, n_new, 2 * A_WIDTH)
        q4 = res[g].reshape(n_seq, n_new, A_HEADS, HEAD_DIM).transpose(0, 2, 1, 3)
        qbd = (q4[:, :, :, None, :] * eye[None, :, None, :, None]).reshape(
            n_seq, A_HEADS * n_new, A_WIDTH).astype(BF16)
        o, lse = _attn_a_sample_t(qbd, states[g], j * n_seq, kvn, dil, window)
        os_.append(o.reshape(n_seq * n_new, A_WIDTH))
        lses.append(lse.reshape(n_seq * n_new, A_WIDTH))
        new_bufs.append(_set_tail_t(bufs[g], kvn, j * n_seq))
    o = _combine_a(os_, lses, [1] * len(A_GROUPS), x.shape[0])
    return o, new_bufs


def _lambda_of(lam_ref, lam_init):
    lp = lam_ref[...]
    a = jnp.sum(lp[0:1, :] * lp[1:2, :], axis=1, keepdims=True)
    b = jnp.sum(lp[2:3, :] * lp[3:4, :], axis=1, keepdims=True)
    return jnp.exp(a) - jnp.exp(b) + lam_init


def _sub_ln(of, g, lam_init):
    ms = jnp.mean(of * of, axis=-1, keepdims=True)
    return of * lax.rsqrt(ms + LN_EPS) * g * (1.0 - lam_init)


def _init_softmax_state(m_sc, l_sc, acc_sc):
    m_sc[...] = jnp.full(m_sc.shape, NEG_INF, F32)
    l_sc[...] = jnp.zeros(l_sc.shape, F32)
    acc_sc[...] = jnp.zeros(acc_sc.shape, F32)


def _softmax_step_t(s, v, m_sc, l_sc, acc_sc, may_be_empty=False, cols=slice(None)):
    m_old = m_sc[:, cols]
    m_new = jnp.maximum(m_old, jnp.max(s, axis=0, keepdims=True))
    m_use = jnp.where(m_new == NEG_INF, 0.0, m_new) if may_be_empty else m_new
    alpha = jnp.exp(m_old - m_use)
    p = jnp.exp(s - m_use)
    l_sc[:, cols] = alpha * l_sc[:, cols] + jnp.sum(p, axis=0, keepdims=True)
    acc_sc[:, cols] = alpha * acc_sc[:, cols] + _dot_tn(v, p.astype(BF16))
    m_sc[:, cols] = m_new


def _attn_b_prompt_kernel(qi_tab, ki_tab, q_ref, k_ref, v_ref, lam_ref, g_ref, o_ref,
                          m_sc, l_sc, acc_sc, *, lam_init):
    tq = q_ref.shape[0]
    t = pl.program_id(2)
    qi = qi_tab[t]
    ki = ki_tab[t]

    @pl.when(ki == 0)
    def _():
        _init_softmax_state(m_sc, l_sc, acc_sc)

    def step(diagonal):
        qs = _pair_stack(q_ref[...])
        s = _dot_nt(k_ref[...], qs)
        if diagonal:
            kpos = lax.broadcasted_iota(jnp.int32, s.shape, 0)
            qpos = lax.broadcasted_iota(jnp.int32, s.shape, 1) % tq
            s = jnp.where(kpos <= qpos, s, NEG_INF)
        _softmax_step_t(s, v_ref[...], m_sc, l_sc, acc_sc)

    @pl.when(ki < qi)
    def _():
        step(False)

    @pl.when(ki == qi)
    def _():
        step(True)
        lam = _lambda_of(lam_ref, lam_init)
        o = acc_sc[...] / l_sc[...]
        of = o[:, :tq] - lam * o[:, tq:]
        ms = jnp.mean(of * of, axis=0, keepdims=True)
        of = jnp.transpose(of * lax.rsqrt(ms + LN_EPS))
        o_ref[...] = (of * g_ref[...] * (1.0 - lam_init)).astype(o_ref.dtype)


def _attn_b_prompt(q, kv, lam_p, subln_g, lam_init, tq):
    b, l, hw = q.shape
    nh = hw // B_VDIM
    nq = l // tq
    pairs = [(i, k) for i in range(nq) for k in range(i + 1)]
    qi_tab = jnp.array([p[0] for p in pairs], jnp.int32)
    ki_tab = jnp.array([p[1] for p in pairs], jnp.int32)
    kernel = functools.partial(_attn_b_prompt_kernel, lam_init=lam_init)
    grid_spec = pltpu.PrefetchScalarGridSpec(
        num_scalar_prefetch=2,
        grid=(b, nh, len(pairs)),
        in_specs=[
            pl.BlockSpec((None, tq, B_VDIM), lambda b_, h, t, qt, kt: (b_, qt[t], h)),
            pl.BlockSpec((None, tq, B_VDIM), lambda b_, h, t, qt, kt: (b_, kt[t], h)),
            pl.BlockSpec((None, tq, B_VDIM), lambda b_, h, t, qt, kt: (b_, kt[t], nh + h)),
            pl.BlockSpec((4, HEAD_DIM), lambda *_: (0, 0)),
            pl.BlockSpec((1, B_VDIM), lambda *_: (0, 0)),
        ],
        out_specs=pl.BlockSpec((None, tq, B_VDIM), lambda b_, h, t, qt, kt: (b_, qt[t], h)),
        scratch_shapes=[pltpu.VMEM((1, 2 * tq), F32), pltpu.VMEM((1, 2 * tq), F32),
                        pltpu.VMEM((B_VDIM, 2 * tq), F32)],
    )
    return pl.pallas_call(
        kernel,
        out_shape=jax.ShapeDtypeStruct(q.shape, BF16),
        grid_spec=grid_spec,
        compiler_params=_cparams(("arbitrary",) * 3),
        name="diff_attn_prompt",
    )(qi_tab, ki_tab, q, kv, kv, lam_p, subln_g.reshape(1, B_VDIM))


def _page_specs(n_per_step, n_pages, page_base, block, tail=None):
    n_steps = n_pages // n_per_step
    tail = (0,) * (len(block) - 1) if tail is None else tail

    def spec(e):
        def index(b, p, pt):
            page = jnp.minimum(p, n_steps - 1) * n_per_step + e
            return (page_base + pt[b * n_pages + page],) + tail
        return pl.BlockSpec(block, index)

    return [spec(e) for e in range(n_per_step)]


def _pad_rows(a, rows):
    return jnp.concatenate([a, jnp.zeros((rows - a.shape[0], a.shape[1]), a.dtype)], axis=0)


B_PAGES_PER_STEP = 4


def _attn_b_sample_kernel(pt_ref, wq_ref, *refs, lam_init, n_new, n_per_step):
    k_refs, v_refs = refs[:n_per_step], refs[n_per_step:2 * n_per_step]
    kn_ref, vn_ref, lam_ref, g_ref, o_ref, m_sc, l_sc, acc_sc = refs[2 * n_per_step:]
    nh = k_refs[0].shape[1]
    cols = wq_ref.shape[0]
    p_id = pl.program_id(1)
    n_steps = pl.num_programs(1) - 1
    same_head = (lax.broadcasted_iota(jnp.int32, (nh, cols), 0)
                 == lax.broadcasted_iota(jnp.int32, (nh, cols), 1) // (2 * n_new))

    def flat(ref_list):
        return jnp.concatenate([r[...].reshape(-1, B_VDIM) for r in ref_list], axis=0).astype(BF16)

    def scores(k, admit):
        s = _dot_nt(k, wq_ref[...])
        s3 = s.reshape(s.shape[0] // nh, nh, cols)
        return jnp.where(admit, s3, NEG_INF).reshape(s.shape)

    @pl.when(p_id == 0)
    def _():
        _init_softmax_state(m_sc, l_sc, acc_sc)

    @pl.when(p_id < n_steps)
    def _():
        _softmax_step_t(scores(flat(k_refs), same_head[None]), flat(v_refs), m_sc, l_sc, acc_sc)

    @pl.when(p_id == n_steps)
    def _():
        jk = lax.broadcasted_iota(jnp.int32, (n_new, nh, cols), 0)
        jq = lax.broadcasted_iota(jnp.int32, (n_new, nh, cols), 2) % n_new
        admit = jnp.logical_and(same_head[None], jk <= jq)
        _softmax_step_t(scores(flat([kn_ref]), admit), flat([vn_ref]), m_sc, l_sc, acc_sc)
        lam = _lambda_of(lam_ref, lam_init)
        o = jnp.transpose(acc_sc[...] / l_sc[...])
        for h in range(nh):
            r0 = h * 2 * n_new
            of = o[r0:r0 + n_new, :] - lam * o[r0 + n_new:r0 + 2 * n_new, :]
            o_ref[:, h * B_VDIM:(h + 1) * B_VDIM] = _sub_ln(of, g_ref[...], lam_init).astype(
                o_ref.dtype)


def _attn_b_sample(wq, pool, page_base, kvn, page_table, lam_p, subln_g, lam_init):
    s, cols, _ = wq.shape
    n_new, nh = kvn.shape[1], kvn.shape[3]
    n_pages = page_table.shape[1]
    pps = B_PAGES_PER_STEP
    assert n_pages % pps == 0 and cols == LANES
    kernel = functools.partial(_attn_b_sample_kernel, lam_init=lam_init, n_new=n_new,
                               n_per_step=pps)
    page_blk = (None, PAGE_SIZE, None, nh, B_VDIM)
    new_blk = (None, n_new, None, nh, B_VDIM)
    grid_spec = pltpu.PrefetchScalarGridSpec(
        num_scalar_prefetch=1,
        grid=(s, n_pages // pps + 1),
        in_specs=[pl.BlockSpec((None, cols, B_VDIM), lambda b, p, pt: (b, 0, 0))]
        + _page_specs(pps, n_pages, page_base, page_blk, (0, 0, 0, 0))
        + _page_specs(pps, n_pages, page_base, page_blk, (0, 1, 0, 0))
        + [
            pl.BlockSpec(new_blk, lambda b, p, pt: (b, 0, 0, 0, 0)),
            pl.BlockSpec(new_blk, lambda b, p, pt: (b, 0, 1, 0, 0)),
            pl.BlockSpec((4, HEAD_DIM), lambda *_: (0, 0)),
            pl.BlockSpec((1, B_VDIM), lambda *_: (0, 0)),
        ],
        out_specs=pl.BlockSpec((None, n_new, nh * B_VDIM), lambda b, p, pt: (b, 0, 0)),
        scratch_shapes=[pltpu.VMEM((1, cols), F32), pltpu.VMEM((1, cols), F32),
                        pltpu.VMEM((B_VDIM, cols), F32)],
    )
    return pl.pallas_call(
        kernel,
        out_shape=jax.ShapeDtypeStruct((s, n_new, nh * B_VDIM), BF16),
        grid_spec=grid_spec,
        compiler_params=_cparams(("arbitrary", "arbitrary")),
        name="diff_attn_sample",
    )(page_table.reshape(-1), wq, *([pool] * (2 * pps)), kvn, kvn, lam_p,
      subln_g.reshape(1, B_VDIM))


def _plan_b():
    hw = B_HEADS * B_VDIM
    half = hw // 2
    return [
        (0, half, half, QK_SCALE, ((0, 0),)),
        (half, half, half, QK_SCALE, ((0, half),)),
        (hw, half, half, 1.0, ((1, 0), (2, 0))),
        (hw + half, half, half, 1.0, ((1, half), (2, half))),
        (2 * hw, half, 0, 1.0, ((1, hw), (2, hw))),
        (2 * hw + half, half, 0, 1.0, ((1, hw + half), (2, hw + half))),
    ]


def _mixer_b_prompt(x, mod3, tables, w_in, lam_p, subln_g, lam_init, bsz, seq_len, tm):
    hw = B_HEADS * B_VDIM
    outs = [(hw, BF16, None), (2 * hw, BF16, None), (2 * hw, F32, None)]
    q, kvb, kvf = _inproj(x, mod3, tables, w_in, _plan_b(), outs, tm, seq_len)
    o = _attn_b_prompt(q.reshape(bsz, seq_len, hw), kvb.reshape(bsz, seq_len, 2 * hw),
                       lam_p, subln_g, lam_init, min(512, seq_len))
    kv = kvf.reshape(bsz * seq_len // PAGE_SIZE, PAGE_SIZE, 2, B_HEADS, B_VDIM)
    return o.reshape(bsz * seq_len, hw), kv


def _mixer_b_sample(x, mod3, tables, w_in, lam_p, subln_g, lam_init, pools, j, page_table,
                    n_seq, n_new):
    hw = B_HEADS * B_VDIM
    outs = [(hw, BF16, None), (2 * hw, F32, None), (2 * hw, F32, None)]
    q, _, kvf = _inproj(x, mod3, tables, w_in, _plan_b(), outs, x.shape[0], n_new)
    q4 = q.reshape(n_seq, n_new, B_HEADS, 2, HEAD_DIM)
    eye_2 = jnp.eye(2, dtype=BF16)
    wq = (q4.transpose(0, 2, 3, 1, 4)[:, :, :, :, None, :]
          * eye_2[None, None, :, None, :, None]).reshape(n_seq, B_HEADS * 2 * n_new, B_VDIM)
    kvn = kvf.reshape(n_seq, n_new, 2, B_HEADS, B_VDIM)
    n_phys = pools.shape[1]
    pool = pools.reshape((pools.shape[0] * n_phys,) + pools.shape[2:])
    o = _attn_b_sample(wq, pool, j * n_phys, kvn, page_table, lam_p, subln_g, lam_init)
    return o.reshape(n_seq * n_new, hw), kvn


def _sortable_key(score):
    bits = lax.bitcast_convert_type(score + 0.0, jnp.int32)
    return bits ^ ((bits >> 31) & jnp.int32(0x7FFFFFFF))


def _kth_largest_key(count_ge, shape, k):
    t0 = jnp.full(shape, INT_MIN, jnp.int32)
    t0 = jnp.where(count_ge(jnp.zeros(shape, jnp.int32)) >= k, jnp.zeros(shape, jnp.int32), t0)

    def body(it, t):
        cand = t + jnp.left_shift(jnp.int32(1), 30 - it)
        return jnp.where(count_ge(cand) >= k, cand, t)

    return lax.fori_loop(0, 31, body, t0)


def _select_prompt_kernel(qi_ref, kiw_all_ref, kiw_q_ref, sel_ref, sc_ref, *, topk):
    tq = qi_ref.shape[0]
    l = kiw_all_ref.shape[0]
    i = pl.program_id(1)
    tk = min(1024, l)
    n_pairs = qi_ref.shape[1] // LANES
    wi = kiw_q_ref[:, IDX_DIM:IDX_DIM + IDX_HEADS] * IDX_W_SCALE
    qs = jnp.concatenate([_pair_stack(qi_ref[:, p * LANES:(p + 1) * LANES])
                          for p in range(n_pairs)], axis=0)
    for c in range(l // tk):
        kic = kiw_all_ref[c * tk:(c + 1) * tk, :]
        lane = lax.broadcasted_iota(jnp.int32, kic.shape, 1)
        ki2 = jnp.where(lane < IDX_DIM, kic, pltpu.roll(kic, IDX_DIM, 1)).astype(BF16)
        r = jnp.maximum(_dot_nt(qs, ki2), 0.0)
        sc = jnp.zeros((tq, tk), F32)
        for h in range(IDX_HEADS):
            sc = sc + r[h * tq:(h + 1) * tq, :] * wi[:, h:h + 1]
        row = lax.broadcasted_iota(jnp.int32, sc.shape, 0) + i * tq
        col = lax.broadcasted_iota(jnp.int32, sc.shape, 1) + c * tk
        sc_ref[:, c * tk:(c + 1) * tk] = _sortable_key(jnp.where(col <= row, sc, NEG_INF))

    def count_ge(t):
        return jnp.sum((sc_ref[...] >= t).astype(jnp.int32), axis=1, keepdims=True)

    thr = _kth_largest_key(count_ge, (tq, 1), topk)
    n_gt = jnp.sum((sc_ref[...] > thr).astype(jnp.int32), axis=1, keepdims=True)
    need = (topk - n_gt).astype(F32)
    tri = (lax.broadcasted_iota(jnp.int32, (LANES, LANES), 0)
           < lax.broadcasted_iota(jnp.int32, (LANES, LANES), 1)).astype(BF16)
    offs = jnp.zeros((tq, 1), F32)
    row = lax.broadcasted_iota(jnp.int32, (tq, LANES), 0) + i * tq
    lane = lax.broadcasted_iota(jnp.int32, (tq, LANES), 1)
    for c in range(l // LANES):
        keys = sc_ref[:, c * LANES:(c + 1) * LANES]
        causal = lane + c * LANES <= row
        eq = jnp.logical_and(keys == thr, causal)
        e = jnp.where(eq, 1.0, 0.0)
        rank = _dot(e.astype(BF16), tri) + offs
        sel = jnp.logical_or(keys > thr, jnp.logical_and(eq, rank < need))
        sel_ref[:, c * LANES:(c + 1) * LANES] = jnp.where(sel, 1.0, 0.0).astype(sel_ref.dtype)
        offs = offs + jnp.sum(e, axis=1, keepdims=True)


def _select_prompt(qi, kiw, topk):
    b, l, w = qi.shape
    tq = PAGE_SIZE
    return pl.pallas_call(
        functools.partial(_select_prompt_kernel, topk=topk),
        out_shape=jax.ShapeDtypeStruct((b, l, l), BF16),
        grid=(b, l // tq),
        in_specs=[
            pl.BlockSpec((None, tq, w), lambda b_, i: (b_, i, 0)),
            pl.BlockSpec((None, l, LANES), lambda b_, i: (b_, 0, 0)),
            pl.BlockSpec((None, tq, LANES), lambda b_, i: (b_, i, 0)),
        ],
        out_specs=pl.BlockSpec((None, tq, l), lambda b_, i: (b_, i, 0)),
        scratch_shapes=[pltpu.VMEM((tq, l), jnp.int32)],
        compiler_params=_cparams(("arbitrary", "arbitrary")),
        name="indexer_select_prompt",
    )(qi, kiw, kiw)


def _attn_c_prompt_kernel(q_ref, k_ref, v_ref, sel_ref, o_ref, m_sc, l_sc, acc_sc):
    tq = q_ref.shape[0]
    n_q_chunks = q_ref.shape[1] // LANES
    per_kv = n_q_chunks // (k_ref.shape[1] // LANES)
    qi = pl.program_id(1)
    ki = pl.program_id(2)
    tk = k_ref.shape[0]
    last = (qi * tq + tq - 1) // tk

    @pl.when(ki == 0)
    def _():
        m_sc[...] = jnp.full(m_sc.shape, NEG_INF, F32)
        l_sc[...] = jnp.zeros(l_sc.shape, F32)
        acc_sc[...] = jnp.zeros(acc_sc.shape, F32)

    @pl.when(ki <= last)
    def _():
        sel = sel_ref[...].astype(F32)
        rows_kv = 2 * per_kv * tq
        selx = jnp.concatenate([sel] * (2 * per_kv), axis=0) > 0.5
        for a in range(k_ref.shape[1] // LANES):
            qs = jnp.concatenate(
                [_pair_stack(q_ref[:, (a * per_kv + e) * LANES:(a * per_kv + e + 1) * LANES])
                 for e in range(per_kv)], axis=0)
            s = _dot_nt(qs, k_ref[:, a * LANES:(a + 1) * LANES])
            s = jnp.where(selx, s, NEG_INF)
            rs = slice(a * rows_kv, (a + 1) * rows_kv)
            m_old = m_sc[rs, :]
            m_new = jnp.maximum(m_old, jnp.max(s, axis=1, keepdims=True))
            m_use = jnp.where(m_new == NEG_INF, 0.0, m_new)
            alpha = jnp.exp(m_old - m_use)
            p = jnp.exp(s - m_use)
            l_sc[rs, :] = alpha * l_sc[rs, :] + jnp.sum(p, axis=1, keepdims=True)
            acc_sc[rs, :] = alpha * acc_sc[rs, :] + _dot(p.astype(BF16),
                                                         v_ref[:, a * LANES:(a + 1) * LANES])
            m_sc[rs, :] = m_new

    @pl.when(ki == pl.num_programs(2) - 1)
    def _():
        o = acc_sc[...] / l_sc[...]
        for c in range(n_q_chunks):
            top = o[(2 * c) * tq:(2 * c + 1) * tq, :]
            bot = o[(2 * c + 1) * tq:(2 * c + 2) * tq, :]
            o_ref[:, c * LANES:(c + 1) * LANES] = _pair_merge(top, bot).astype(o_ref.dtype)


def _attn_c_prompt(q, kd, vd, sel, tk):
    b, l, qw = q.shape
    tq = PAGE_SIZE
    kw = kd.shape[2]
    nk = l // tk
    rows = 2 * (qw // LANES) * tq
    last = lambda i: (i * tq + tq - 1) // tk
    return pl.pallas_call(
        _attn_c_prompt_kernel,
        out_shape=jax.ShapeDtypeStruct(q.shape, BF16),
        grid=(b, l // tq, nk),
        in_specs=[
            pl.BlockSpec((None, tq, qw), lambda b_, i, k: (b_, i, 0)),
            pl.BlockSpec((None, tk, kw), lambda b_, i, k: (b_, jnp.minimum(k, last(i)), 0)),
            pl.BlockSpec((None, tk, kw), lambda b_, i, k: (b_, jnp.minimum(k, last(i)), 0)),
            pl.BlockSpec((None, tq, tk), lambda b_, i, k: (b_, i, jnp.minimum(k, last(i)))),
        ],
        out_specs=pl.BlockSpec((None, tq, qw), lambda b_, i, k: (b_, i, 0)),
        scratch_shapes=[pltpu.VMEM((rows, 1), F32), pltpu.VMEM((rows, 1), F32),
                        pltpu.VMEM((rows, LANES), F32)],
        compiler_params=_cparams(("arbitrary",) * 3),
        name="sparse_attn_prompt",
    )(q, kd, vd, sel)


KEY_NEG_INF = -2139095041
SELECT_ROW_CHUNK = 512


def _select_prompt_t_kernel(qi_ref, kiw_all_ref, kiw_q_ref, sel_ref, key_sc, *, topk):
    tq = qi_ref.shape[0]
    l = kiw_all_ref.shape[0]
    i = pl.program_id(1)
    rc = min(SELECT_ROW_CHUNK, l)
    n_pairs = qi_ref.shape[1] // LANES
    q_hi = i * tq + tq - 1
    wi_t = jnp.transpose(kiw_q_ref[...])[IDX_DIM:IDX_DIM + IDX_HEADS, :] * IDX_W_SCALE
    qs = jnp.concatenate([_pair_stack(qi_ref[:, p * LANES:(p + 1) * LANES])
                          for p in range(n_pairs)], axis=0)
    qpos = lax.broadcasted_iota(jnp.int32, (rc, tq), 1) + i * tq
    krow = lax.broadcasted_iota(jnp.int32, (rc, tq), 0)
    for c in range(l // rc):
        rows = slice(c * rc, (c + 1) * rc)

        @pl.when(c * rc <= q_hi)
        def _(c=c, rows=rows):
            kic = kiw_all_ref[rows, :]
            lane = lax.broadcasted_iota(jnp.int32, kic.shape, 1)
            ki2 = jnp.where(lane < IDX_DIM, kic, pltpu.roll(kic, IDX_DIM, 1)).astype(BF16)
            r = jnp.maximum(_dot_nt(ki2, qs), 0.0)
            sc = jnp.zeros((rc, tq), F32)
            for h in range(IDX_HEADS):
                sc = sc + r[:, h * tq:(h + 1) * tq] * wi_t[h:h + 1, :]
            key_sc[rows, :] = _sortable_key(jnp.where(krow + c * rc <= qpos, sc, NEG_INF))

        @pl.when(c * rc > q_hi)
        def _(rows=rows):
            key_sc[rows, :] = jnp.full((rc, tq), KEY_NEG_INF, jnp.int32)

    n_scan = q_hi // rc + 1

    def count(pred):
        def body(c, acc):
            k = key_sc[pl.ds(pl.multiple_of(c * rc, rc), rc), :]
            return acc + jnp.sum(jnp.where(pred(k), 1, 0).reshape(rc // 8, 8, tq), axis=0)
        acc = lax.fori_loop(0, n_scan, body, jnp.zeros((8, tq), jnp.int32))
        return jnp.sum(acc, axis=0, keepdims=True)

    def count_ge(t):
        return count(lambda k: k >= t) + jnp.where(t <= KEY_NEG_INF, l - n_scan * rc, 0)

    thr = _kth_largest_key(count_ge, (1, tq), topk)
    need = (topk - count(lambda k: k > thr)).astype(F32)
    tri = (lax.broadcasted_iota(jnp.int32, (LANES, LANES), 1)
           < lax.broadcasted_iota(jnp.int32, (LANES, LANES), 0)).astype(BF16)
    qpos = lax.broadcasted_iota(jnp.int32, (LANES, tq), 1) + i * tq
    krow = lax.broadcasted_iota(jnp.int32, (LANES, tq), 0)
    offs = jnp.zeros((1, tq), F32)
    for c in range(l // LANES):
        keys = key_sc[c * LANES:(c + 1) * LANES, :]
        eq = jnp.logical_and(keys == thr, krow + c * LANES <= qpos)
        e = jnp.where(eq, 1.0, 0.0)
        rank = _dot(tri, e.astype(BF16)) + offs
        sel = jnp.logical_or(keys > thr, jnp.logical_and(eq, rank < need))
        sel_ref[c * LANES:(c + 1) * LANES, :] = jnp.where(sel, 1.0, 0.0).astype(sel_ref.dtype)
        offs = offs + jnp.sum(e, axis=0, keepdims=True)


def _select_prompt_t(qi, kiw, topk):
    b, l, w = qi.shape
    tq = PAGE_SIZE
    return pl.pallas_call(
        functools.partial(_select_prompt_t_kernel, topk=topk),
        out_shape=jax.ShapeDtypeStruct((b, l, l), BF16),
        grid=(b, l // tq),
        in_specs=[
            pl.BlockSpec((None, tq, w), lambda b_, i: (b_, i, 0)),
            pl.BlockSpec((None, l, LANES), lambda b_, i: (b_, 0, 0)),
            pl.BlockSpec((None, tq, LANES), lambda b_, i: (b_, i, 0)),
        ],
        out_specs=pl.BlockSpec((None, l, tq), lambda b_, i: (b_, 0, i)),
        scratch_shapes=[pltpu.VMEM((l, tq), jnp.int32)],
        compiler_params=_cparams(("arbitrary", "arbitrary")),
        name="indexer_select_prompt",
    )(qi, kiw, kiw)


def _attn_c_prompt_t_kernel(qi_tab, ki_tab, q_ref, k_ref, v_ref, sel_ref, o_ref,
                            m_sc, l_sc, acc_sc):
    tq = q_ref.shape[0]
    tk = k_ref.shape[0]
    n_kv = k_ref.shape[1] // LANES
    per_kv = q_ref.shape[1] // LANES // n_kv
    cols_kv = 2 * per_kv * tq
    t = pl.program_id(1)
    qi = qi_tab[t]
    ki = ki_tab[t]

    @pl.when(ki == 0)
    def _():
        _init_softmax_state(m_sc, l_sc, acc_sc)

    sel = sel_ref[...].astype(F32)
    selx = jnp.concatenate([sel] * (2 * per_kv), axis=1) > 0.5
    for a in range(n_kv):
        qs = jnp.concatenate(
            [_pair_stack(q_ref[:, (a * per_kv + e) * LANES:(a * per_kv + e + 1) * LANES])
             for e in range(per_kv)], axis=0)
        s = jnp.where(selx, _dot_nt(k_ref[:, a * LANES:(a + 1) * LANES], qs), NEG_INF)
        _softmax_step_t(s, v_ref[:, a * LANES:(a + 1) * LANES], m_sc, l_sc, acc_sc,
                        may_be_empty=True, cols=slice(a * cols_kv, (a + 1) * cols_kv))

    @pl.when(ki == (qi * tq + tq - 1) // tk)
    def _():
        o = acc_sc[...] / l_sc[...]
        for c in range(n_kv * per_kv):
            lo = o[0:HEAD_DIM, (2 * c) * tq:(2 * c + 1) * tq]
            hi = o[0:HEAD_DIM, (2 * c + 1) * tq:(2 * c + 2) * tq]
            o_ref[:, c * LANES:(c + 1) * LANES] = jnp.transpose(
                jnp.concatenate([lo, hi], axis=0)).astype(o_ref.dtype)


def _attn_c_prompt_t(q, kd, vd, sel_t, tk):
    b, l, qw = q.shape
    tq = PAGE_SIZE
    kw = kd.shape[2]
    cols = 2 * (qw // LANES) * tq
    pairs = [(i, k) for i in range(l // tq) for k in range((i * tq + tq - 1) // tk + 1)]
    qi_tab = jnp.array([p[0] for p in pairs], jnp.int32)
    ki_tab = jnp.array([p[1] for p in pairs], jnp.int32)
    grid_spec = pltpu.PrefetchScalarGridSpec(
        num_scalar_prefetch=2,
        grid=(b, len(pairs)),
        in_specs=[
            pl.BlockSpec((None, tq, qw), lambda b_, t, qt, kt: (b_, qt[t], 0)),
            pl.BlockSpec((None, tk, kw), lambda b_, t, qt, kt: (b_, kt[t], 0)),
            pl.BlockSpec((None, tk, kw), lambda b_, t, qt, kt: (b_, kt[t], 0)),
            pl.BlockSpec((None, tk, tq), lambda b_, t, qt, kt: (b_, kt[t], qt[t])),
        ],
        out_specs=pl.BlockSpec((None, tq, qw), lambda b_, t, qt, kt: (b_, qt[t], 0)),
        scratch_shapes=[pltpu.VMEM((1, cols), F32), pltpu.VMEM((1, cols), F32),
                        pltpu.VMEM((LANES, cols), F32)],
    )
    return pl.pallas_call(
        _attn_c_prompt_t_kernel,
        out_shape=jax.ShapeDtypeStruct(q.shape, BF16),
        grid_spec=grid_spec,
        compiler_params=_cparams(("arbitrary", "arbitrary")),
        name="sparse_attn_prompt",
    )(qi_tab, ki_tab, q, kd, vd, sel_t)


C_PAGES_PER_STEP = 8
C_SELECT_PAGES_PER_STEP = 16


def _select_sample_kernel(pt_ref, qs_ref, wi_ref, *refs, topk, n_new, n_per_step):
    page_refs = refs[:n_per_step]
    kin_ref, sel_ref, key_sc = refs[n_per_step:]
    p_id = pl.program_id(1)
    n_steps = pl.num_programs(1) - 1
    n_pages = key_sc.shape[0] - 1
    wi = wi_ref[...] * IDX_W_SCALE

    def scores(ki_t):
        r = jnp.maximum(_dot(qs_ref[...], ki_t), 0.0)
        sc = jnp.zeros((n_new, PAGE_SIZE), F32)
        for h in range(IDX_HEADS):
            sc = sc + r[h * n_new:(h + 1) * n_new, :] * wi[:, h:h + 1]
        return sc

    @pl.when(p_id < n_steps)
    def _():
        for e, page_ref in enumerate(page_refs):
            key_sc[p_id * n_per_step + e] = _sortable_key(scores(page_ref[...].astype(BF16)))

    @pl.when(p_id == n_steps)
    def _():
        for e in range(1, sel_ref.shape[0] - n_pages):
            sel_ref[n_pages + e] = jnp.zeros((n_new, PAGE_SIZE), F32)
        sc = scores(jnp.transpose(_pad_rows(kin_ref[...], PAGE_SIZE)).astype(BF16))
        jq = lax.broadcasted_iota(jnp.int32, sc.shape, 0)
        jk = lax.broadcasted_iota(jnp.int32, sc.shape, 1)
        key_sc[n_pages] = _sortable_key(jnp.where(jk <= jq, sc, NEG_INF))

        def count(pred):
            c = jnp.where(pred(key_sc[...]), 1, 0)
            return jnp.sum(jnp.sum(c, axis=0), axis=1, keepdims=True)

        thr = _kth_largest_key(lambda t: count(lambda k: k >= t[None]), (n_new, 1), topk)
        need = (topk - count(lambda k: k > thr[None])).astype(F32)
        tri = (lax.broadcasted_iota(jnp.int32, (LANES, LANES), 0)
               < lax.broadcasted_iota(jnp.int32, (LANES, LANES), 1)).astype(BF16)

        def body(p, offs):
            keys = key_sc[p]
            valid = jnp.logical_or(p < n_pages, jk <= jq)
            eq = jnp.logical_and(keys == thr, valid)
            e = jnp.where(eq, 1.0, 0.0)
            rank = _dot(e.astype(BF16), tri) + offs
            sel = jnp.logical_or(keys > thr, jnp.logical_and(eq, rank < need))
            sel_ref[p] = jnp.where(sel, 1.0, 0.0)
            return offs + jnp.sum(e, axis=1, keepdims=True)

        lax.fori_loop(0, n_pages + 1, body, jnp.zeros((n_new, 1), F32))


def _select_sample(qs, wi, pool_idx, page_base, ki_new, page_table, topk):
    s, rows, _ = qs.shape
    n_new = ki_new.shape[1]
    n_pages = page_table.shape[1]
    pps = min(C_SELECT_PAGES_PER_STEP, n_pages)
    n_sel = n_pages + C_PAGES_PER_STEP
    assert n_pages % pps == 0
    kernel = functools.partial(_select_sample_kernel, topk=topk, n_new=n_new, n_per_step=pps)
    grid_spec = pltpu.PrefetchScalarGridSpec(
        num_scalar_prefetch=1,
        grid=(s, n_pages // pps + 1),
        in_specs=[
            pl.BlockSpec((None, rows, IDX_DIM), lambda b, p, pt: (b, 0, 0)),
            pl.BlockSpec((None, n_new, IDX_HEADS), lambda b, p, pt: (b, 0, 0)),
        ] + _page_specs(pps, n_pages, page_base, (None, IDX_DIM, PAGE_SIZE)) + [
            pl.BlockSpec((None, n_new, IDX_DIM), lambda b, p, pt: (b, 0, 0)),
        ],
        out_specs=pl.BlockSpec((None, n_sel, n_new, PAGE_SIZE), lambda b, p, pt: (b, 0, 0, 0)),
        scratch_shapes=[pltpu.VMEM((n_pages + 1, n_new, PAGE_SIZE), jnp.int32)],
    )
    return pl.pallas_call(
        kernel,
        out_shape=jax.ShapeDtypeStruct((s, n_sel, n_new, PAGE_SIZE), F32),
        grid_spec=grid_spec,
        compiler_params=_cparams(("arbitrary", "arbitrary")),
        name="indexer_select_sample",
    )(page_table.reshape(-1), qs, wi, *([pool_idx] * pps), ki_new)


def _attn_c_sample_kernel(pt_ref, wq_ref, *refs, n_new, n_per_step):
    page_refs = refs[:n_per_step]
    kvn_ref, sel_ref, o_ref, m_sc, l_sc, acc_sc = refs[n_per_step:]
    kw = wq_ref.shape[1]
    n_rep = wq_ref.shape[0] // n_new
    p_id = pl.program_id(1)
    n_steps = pl.num_programs(1) - 1

    @pl.when(p_id == 0)
    def _():
        _init_softmax_state(m_sc, l_sc, acc_sc)

    def update(k_t, v_t, sel):
        sel = jnp.concatenate([sel] * n_rep, axis=0) > 0.5
        s = jnp.where(sel, _dot(wq_ref[...], k_t), NEG_INF)
        m_old = m_sc[...]
        m_new = jnp.maximum(m_old, jnp.max(s, axis=1, keepdims=True))
        m_use = jnp.where(m_new == NEG_INF, 0.0, m_new)
        alpha = jnp.exp(m_old - m_use)
        p = jnp.exp(s - m_use)
        l_sc[...] = alpha * l_sc[...] + jnp.sum(p, axis=1, keepdims=True)
        acc_sc[...] = alpha * acc_sc[...] + _dot_nt(p.astype(BF16), v_t)
        m_sc[...] = m_new

    @pl.when(p_id < n_steps)
    def _():
        update(jnp.concatenate([r[0].astype(BF16) for r in page_refs], axis=1),
               jnp.concatenate([r[1].astype(BF16) for r in page_refs], axis=1),
               jnp.concatenate([sel_ref[e] for e in range(n_per_step)], axis=1))

    @pl.when(p_id == n_steps)
    def _():
        new_t = lambda a: jnp.transpose(_pad_rows(a, PAGE_SIZE)).astype(BF16)
        update(new_t(kvn_ref[:, 0:kw]), new_t(kvn_ref[:, kw:2 * kw]), sel_ref[0])
        o_ref[...] = acc_sc[...] / l_sc[...]


def _attn_c_sample(wq, pool, page_base, kvn, sel, page_table):
    s, rows, kw = wq.shape
    n_new = kvn.shape[1]
    n_pages = page_table.shape[1]
    pps = C_PAGES_PER_STEP
    grid_spec = pltpu.PrefetchScalarGridSpec(
        num_scalar_prefetch=1,
        grid=(s, n_pages // pps + 1),
        in_specs=[pl.BlockSpec((None, rows, kw), lambda b, p, pt: (b, 0, 0))]
        + _page_specs(pps, n_pages, page_base, (None, 2, kw, PAGE_SIZE))
        + [
            pl.BlockSpec((None, n_new, 2 * kw), lambda b, p, pt: (b, 0, 0)),
            pl.BlockSpec((None, pps, n_new, PAGE_SIZE), lambda b, p, pt: (b, p, 0, 0)),
        ],
        out_specs=pl.BlockSpec((None, rows, kw), lambda b, p, pt: (b, 0, 0)),
        scratch_shapes=[pltpu.VMEM((rows, 1), F32), pltpu.VMEM((rows, 1), F32),
                        pltpu.VMEM((rows, kw), F32)],
    )
    return pl.pallas_call(
        functools.partial(_attn_c_sample_kernel, n_new=n_new, n_per_step=pps),
        out_shape=jax.ShapeDtypeStruct((s, rows, kw), F32),
        grid_spec=grid_spec,
        compiler_params=_cparams(("arbitrary", "arbitrary")),
        name="sparse_attn_sample",
    )(page_table.reshape(-1), wq, *([pool] * pps), kvn, sel)


C_QW = C_HEADS * HEAD_DIM
C_KW = C_KV_HEADS * HEAD_DIM
C_IW = IDX_HEADS * IDX_DIM


def _prep_c_weights(w):
    q, k, v, qi, ki, wi = jnp.split(w, [C_QW, C_QW + C_KW, C_QW + 2 * C_KW,
                                        C_QW + 2 * C_KW + C_IW, C_QW + 2 * C_KW + C_IW + IDX_DIM],
                                    axis=1)
    dup = lambda a: jnp.repeat(a.reshape(a.shape[0], C_KV_HEADS, 1, HEAD_DIM), 2, axis=2).reshape(
        a.shape[0], 2 * C_KW)
    pad = jnp.zeros((w.shape[0], LANES - IDX_DIM - IDX_HEADS), w.dtype)
    return jnp.concatenate([q, k, v, qi, ki, wi, pad, dup(k), dup(v)], axis=1)


def _plan_c():
    o_kv = C_QW
    o_qi = C_QW + 2 * C_KW
    o_ki = o_qi + C_IW
    o_kd = o_ki + LANES
    o_vd = o_kd + 2 * C_KW
    return [
        (0, 512, 512, QK_SCALE, ((0, 0),)),
        (512, 512, 512, QK_SCALE, ((0, 512),)),
        (o_kv, 2 * C_KW, C_KW, 1.0, ((1, 0),)),
        (o_qi, C_IW, C_IW, 1.0, ((2, 0),)),
        (o_ki, LANES, IDX_DIM, 1.0, ((3, 0),)),
        (o_kd, 2 * C_KW, 2 * C_KW, 1.0, ((4, 0),)),
        (o_vd, 2 * C_KW, 0, 1.0, ((5, 0),)),
    ]


def _mixer_c_prompt(x, mod3, tables, w_prep, bsz, seq_len, tm):
    outs = [(C_QW, BF16, None), (2 * C_KW, F32, None), (C_IW, BF16, None), (LANES, F32, None),
            (2 * C_KW, BF16, None), (2 * C_KW, BF16, None)]
    q, kvf, qi, kiw, kd, vd = _inproj(x, mod3, tables, w_prep, _plan_c(), outs, tm, seq_len)
    topk = min(C_TOPK_MAX, seq_len // 4)
    r3 = lambda a: a.reshape(bsz, seq_len, a.shape[-1])
    sel_t = _select_prompt_t(r3(qi), r3(kiw), topk)
    o = _attn_c_prompt_t(r3(q), r3(kd), r3(vd), sel_t, min(512, seq_len))
    n_pg = bsz * seq_len // PAGE_SIZE
    kv = kvf.reshape(n_pg, PAGE_SIZE, 2, C_KV_HEADS, HEAD_DIM)
    ki = kiw[:, :IDX_DIM].reshape(n_pg, PAGE_SIZE, IDX_DIM)
    return o.reshape(bsz * seq_len, C_QW), kv, ki


def _mixer_c_sample(x, mod3, tables, w_prep, pools_kv, pools_idx, j, page_table, n_seq, n_new):
    n_phys = pools_kv.shape[1]
    pool_kv = jnp.transpose(pools_kv, (0, 1, 3, 4, 5, 2)).reshape(
        pools_kv.shape[0] * n_phys, 2, C_KW, PAGE_SIZE)
    pool_idx = jnp.transpose(pools_idx, (0, 1, 3, 2)).reshape(
        pools_idx.shape[0] * n_phys, IDX_DIM, PAGE_SIZE)
    outs = [(C_QW, BF16, None), (2 * C_KW, F32, None), (C_IW, BF16, None), (LANES, F32, None),
            (2 * C_KW, BF16, None), (2 * C_KW, BF16, None)]
    q, kvf, qi, kiw, _, _ = _inproj(x, mod3, tables, w_prep, _plan_c(), outs, x.shape[0], n_new)
    n_past = page_table.shape[1] * PAGE_SIZE
    topk = min(C_TOPK_MAX, (n_past + n_new) // 4)
    qs = qi.reshape(n_seq, n_new, IDX_HEADS, IDX_DIM).transpose(0, 2, 1, 3).reshape(
        n_seq, IDX_HEADS * n_new, IDX_DIM)
    kiw3 = kiw.reshape(n_seq, n_new, LANES)
    ki_new = kiw3[:, :, :IDX_DIM]
    wi = kiw3[:, :, IDX_DIM:IDX_DIM + IDX_HEADS]
    sel = _select_sample(qs, wi, pool_idx, j * n_phys, ki_new, page_table, topk)
    q4 = q.reshape(n_seq, n_new, C_HEADS, HEAD_DIM).transpose(0, 2, 1, 3)
    kv_of = jnp.arange(C_HEADS) // (C_HEADS // C_KV_HEADS)
    onehot = (kv_of[:, None] == jnp.arange(C_KV_HEADS)[None, :]).astype(BF16)
    wq = (q4[:, :, :, None, :] * onehot[None, :, None, :, None]).reshape(
        n_seq, C_HEADS * n_new, C_KW)
    kvn = kvf.reshape(n_seq, n_new, 2 * C_KW)
    acc = _attn_c_sample(wq, pool_kv, j * n_phys, kvn, sel, page_table)
    acc = acc.reshape(n_seq, C_HEADS, n_new, C_KV_HEADS, HEAD_DIM)
    o = jnp.einsum("shjad,ha->sjhd", acc, onehot.astype(F32))
    o = o.reshape(n_seq * n_new, C_QW).astype(BF16)
    return (o, kvf.reshape(n_seq, n_new, 2, C_KV_HEADS, HEAD_DIM),
            ki_new.reshape(n_seq, n_new, IDX_DIM))


def kernel(x_prompt, x_sample, state_a_kv_w128, state_a_kv_w512, state_a_kv_w2048, cache_b_kv,
           cache_c_kv, cache_c_idx, page_table, c_prompt, c_sample, w_mod, b_mod, ln_g, ln_b,
           w_ffn_in, w_ffn_out, a_w_in, a_w_out, b_w_in, b_w_out, b_lambda, b_subln_g, c_w_in,
           c_w_out):
    bsz, seq_len, d = x_prompt.shape
    n_seq, n_new, _ = x_sample.shape
    n_past = page_table.shape[1] * PAGE_SIZE
    a_states = (state_a_kv_w128, state_a_kv_w512, state_a_kv_w2048)
    tm = min(512, seq_len)
    rs = n_seq * n_new

    n_c = bsz + n_seq
    n_c_pad = -(-n_c // 8) * 8
    c_all = jnp.concatenate([c_prompt, c_sample, jnp.zeros((n_c_pad - n_c, d), F32)], axis=0)
    mod_all = _modulation(c_all, w_mod, b_mod)

    tables_p = _rope_tables(jnp.arange(seq_len, dtype=jnp.int32))
    tables_s = tuple(jnp.tile(t, (n_seq, 1)) for t in
                     _rope_tables(n_past + jnp.arange(n_new, dtype=jnp.int32)))

    xp = x_prompt.reshape(bsz * seq_len, d)
    xs = x_sample.reshape(rs, d)
    new_a_p = [[] for _ in A_GROUPS]
    a_flat = [jnp.transpose(s, (0, 1, 3, 4, 5, 2)).reshape(s.shape[0] * n_seq, 2, A_WIDTH, s.shape[2])
              for s in a_states]
    a_bufs = [_shift_state_t(s, n_new) for s in a_flat]
    b_p, b_s, ckv_p, ckv_s, cidx_p, cidx_s = [], [], [], [], [], []

    for i in range(DEPTH):
        kind, j = i % N_MIXERS, i // N_MIXERS
        mod_p = mod_all[i, :bsz][:, None, :]
        mod_s = jnp.repeat(mod_all[i, bsz:bsz + n_seq], n_new, axis=0)[None]
        if kind == 0:
            w_in = a_w_in[j].astype(BF16)
            w_out = a_w_out[j].astype(BF16)
            op, bufs_p = _mixer_a_prompt(xp, mod_p, tables_p, w_in, bsz, seq_len, tm)
            os_, a_bufs = _mixer_a_sample(xs, mod_s, tables_s, w_in, a_flat, a_bufs, j,
                                          n_seq, n_new)
            for g in range(len(A_GROUPS)):
                new_a_p[g].append(bufs_p[g])
        elif kind == 1:
            w_in = b_w_in[j].astype(BF16)
            w_out = b_w_out[j].astype(BF16)
            lam_init = 0.8 - 0.6 * math.exp(-0.3 * i)
            op, kv_p = _mixer_b_prompt(xp, mod_p, tables_p, w_in, b_lambda[j], b_subln_g[j],
                                       lam_init, bsz, seq_len, tm)
            os_, kv_s = _mixer_b_sample(xs, mod_s, tables_s, w_in, b_lambda[j], b_subln_g[j],
                                        lam_init, cache_b_kv, j, page_table, n_seq, n_new)
            b_p.append(kv_p)
            b_s.append(kv_s)
        else:
            w_in = _prep_c_weights(c_w_in[j]).astype(BF16)
            w_out = c_w_out[j].astype(BF16)
            op, kv_p, ki_p = _mixer_c_prompt(xp, mod_p, tables_p, w_in, bsz, seq_len, tm)
            os_, kv_s, ki_s = _mixer_c_sample(xs, mod_s, tables_s, w_in, cache_c_kv, cache_c_idx,
                                              j, page_table, n_seq, n_new)
            ckv_p.append(kv_p)
            ckv_s.append(kv_s)
            cidx_p.append(ki_p)
            cidx_s.append(ki_s)
        w1 = w_ffn_in[i].astype(BF16)
        w2 = w_ffn_out[i].astype(BF16)
        xp = _post(xp, op, mod_p, w_out, ln_g[i], ln_b[i], w1, w2, tm, seq_len)
        xs = _post(xs, os_, mod_s, w_out, ln_g[i], ln_b[i], w1, w2, rs, n_new)

    y_prompt = xp.reshape(bsz, seq_len, d)
    y_sample = xs.reshape(n_seq, n_new, d)
    stack = lambda xs: xs[0][None] if len(xs) == 1 else jnp.stack(xs)
    a_p = [stack(a) for a in new_a_p]
    a_s = [jnp.transpose(b.reshape(s.shape[0], n_seq, 2, A_HEADS, HEAD_DIM, s.shape[2]),
                         (0, 1, 5, 2, 3, 4)) for b, s in zip(a_bufs, a_states)]
    return (y_prompt, y_sample, a_p[0], a_s[0], a_p[1], a_s[1], a_p[2], a_s[2],
            stack(b_p), stack(b_s), stack(ckv_p), stack(ckv_s), stack(cidx_p), stack(cidx_s))
```
